```python
import jax, jax.numpy as jnp
from jax import lax
import numpy as np

D_MODEL = 2048
BATCH = 2
SEQ = 8192
DEPTH = 4

GRID_W = 64
GLA_HEADS = 4
GLA_DK = D_MODEL // 4
GLA_DV = D_MODEL // 2
GLA_HK = GLA_DK // GLA_HEADS
GLA_HV = GLA_DV // GLA_HEADS
GLA_GATE_RANK = 16
GLA_TAU = 16.0
GLA_CHUNK = 64
NA_HD = 64
NA_WIDTH = D_MODEL // 4
NA_HEADS = NA_WIDTH // NA_HD
NA_ROWS = 8
NA_COLS = 16
NA_QB_W = 16
NA_KB_W = NA_QB_W + NA_COLS
MEM_TOKENS = 256
MEM_HEADS = 4
MEM_WIDTH = D_MODEL // 4
MEM_HD = MEM_WIDTH // MEM_HEADS
MIX_WIDTH = GLA_DV + NA_WIDTH + MEM_WIDTH
IN_SPLITS = (GLA_DK, GLA_DK, GLA_DV, GLA_DV, GLA_GATE_RANK, GLA_GATE_RANK,
             NA_WIDTH, NA_WIDTH, NA_WIDTH, MEM_WIDTH)
IN_COLS = sum(IN_SPLITS)
D_FF = ((8 * D_MODEL // 3 + 255) // 256) * 256
RMS_EPS = 1e-6

kernel_name = "hybrid_gla_natten_mem_encoder"


def rms_norm(x, g):
    xf = x.astype(jnp.float32)
    y = xf * lax.rsqrt(jnp.mean(xf * xf, axis=-1, keepdims=True) + RMS_EPS)
    return (y * g.astype(jnp.float32)).astype(x.dtype)


def head_rms_norm(t, g, n_heads):
    B, S, W = t.shape
    y = rms_norm(t.reshape(B, S, n_heads, W // n_heads), g.reshape(n_heads, W // n_heads))
    return y.reshape(B, S, W)


def split_heads(t, n_heads):
    B, S, W = t.shape
    return t.reshape(B, S, n_heads, W // n_heads).transpose(0, 2, 1, 3)


def merge_heads(t):
    B, H, S, d = t.shape
    return t.transpose(0, 2, 1, 3).reshape(B, S, H * d)


def gla_chunked(q, k, v, log_a, strict):
    B, H, S, dk = q.shape
    dv = v.shape[-1]
    C = GLA_CHUNK
    N = S // C
    qc = q.astype(jnp.float32).reshape(B, H, N, C, dk)
    kc = k.astype(jnp.float32).reshape(B, H, N, C, dk)
    vc = v.astype(jnp.float32).reshape(B, H, N, C, dv)
    cum = jnp.cumsum(log_a.astype(jnp.float32).reshape(B, H, N, C, dk), axis=3)
    last = cum[:, :, :, -1:, :]
    q_e = qc * jnp.exp(cum)
    k_e = kc * jnp.exp(-cum)
    k_end = kc * jnp.exp(last - cum)
    scores = jnp.einsum('bhnqd,bhnkd->bhnqk', q_e, k_e)
    mask = np.tril(np.ones((C, C), dtype=bool), k=-1 if strict else 0)
    scores = jnp.where(mask, scores, 0.0)
    o_intra = jnp.einsum('bhnqk,bhnke->bhnqe', scores, vc)
    kv = jnp.einsum('bhnkd,bhnke->nbhde', k_end, vc)
    decay = jnp.exp(last[:, :, :, 0, :]).transpose(2, 0, 1, 3)

    def step(state, inp):
        dec, kv_n = inp
        return dec[..., None] * state + kv_n, state

    _, s_prev = lax.scan(step, jnp.zeros((B, H, dk, dv), jnp.float32), (decay, kv))
    o_inter = jnp.einsum('bhnqd,nbhde->bhnqe', q_e, s_prev)
    return (o_intra + o_inter).reshape(B, H, S, dv)


def gla_group(q_in, k_in, v_in, r_in, gf_in, gb_in, wg2_f, bg_f, wg2_b, bg_b, out_norm):
    q = split_heads(q_in, GLA_HEADS) * (GLA_HK ** -0.5)
    k = split_heads(k_in, GLA_HEADS)
    v = split_heads(v_in, GLA_HEADS)
    z_f = gf_in.astype(jnp.float32) @ wg2_f.astype(jnp.float32) + bg_f.astype(jnp.float32)
    z_b = gb_in.astype(jnp.float32) @ wg2_b.astype(jnp.float32) + bg_b.astype(jnp.float32)
    la_f = split_heads(jax.nn.log_sigmoid(z_f) / GLA_TAU, GLA_HEADS)
    la_b = split_heads(jax.nn.log_sigmoid(z_b) / GLA_TAU, GLA_HEADS)
    o_f = gla_chunked(q, k, v, la_f, strict=False)
    flip = lambda t: jnp.flip(t, axis=2)
    o_b = flip(gla_chunked(flip(q), flip(k), flip(v), flip(la_b), strict=True))
    o = merge_heads(o_f + o_b).astype(q_in.dtype)
    o = head_rms_norm(o, out_norm, GLA_HEADS)
    return o * jax.nn.silu(r_in)


def natten2d(q, k, v, rpb):
    B, H, S, dh = q.shape
    rows = S // GRID_W
    kr = min(NA_ROWS, rows)
    nj = GRID_W // NA_QB_W
    r = np.arange(rows)
    rs = np.clip(r - kr // 2, 0, rows - kr)
    c0 = np.arange(nj) * NA_QB_W
    kc0 = np.clip(c0 - NA_COLS // 2, 0, GRID_W - NA_KB_W)
    key_rows = rs[:, None] + np.arange(kr)[None, :]
    key_cols = kc0[:, None] + np.arange(NA_KB_W)[None, :]
    nk = kr * NA_KB_W
    idx = (key_rows[:, None, :, None] * GRID_W + key_cols[None, :, None, :]).reshape(-1)
    k_blk = jnp.take(k, idx, axis=2).reshape(B, H, rows, nj, nk, dh)
    v_blk = jnp.take(v, idx, axis=2).reshape(B, H, rows, nj, nk, dh)
    q_blk = q.reshape(B, H, rows, nj, NA_QB_W, dh)
    qcol = c0[:, None] + np.arange(NA_QB_W)[None, :]
    cs = np.clip(qcol - NA_COLS // 2, 0, GRID_W - NA_COLS)
    in_win = (key_cols[:, None, :] >= cs[:, :, None]) & (key_cols[:, None, :] < cs[:, :, None] + NA_COLS)
    mask = np.broadcast_to(in_win[:, :, None, :], (nj, NA_QB_W, kr, NA_KB_W)).reshape(nj, NA_QB_W, nk)
    dr = key_rows - r[:, None] + (NA_ROWS - 1)
    dc = np.clip(key_cols[:, None, :] - qcol[:, :, None], -(NA_COLS - 1), NA_COLS - 1) + (NA_COLS - 1)
    bias = rpb[:, dr[:, None, None, :, None], dc[None, :, :, None, :]].reshape(H, rows, nj, NA_QB_W, nk)
    s = jnp.einsum('bhrjqd,bhrjkd->bhrjqk', q_blk, k_blk).astype(jnp.float32) * (dh ** -0.5)
    s = s + bias[None].astype(jnp.float32)
    s = jnp.where(mask, s, -jnp.inf)
    p = jax.nn.softmax(s, axis=-1).astype(v.dtype)
    o = jnp.einsum('bhrjqk,bhrjkd->bhrjqd', p, v_blk)
    return o.reshape(B, H, S, dh)


def natten_group(q_in, k_in, v_in, q_norm, k_norm, rpb, out_norm):
    q = rms_norm(split_heads(q_in, NA_HEADS), q_norm)
    k = rms_norm(split_heads(k_in, NA_HEADS), k_norm)
    v = split_heads(v_in, NA_HEADS)
    o = merge_heads(natten2d(q, k, v, rpb))
    return head_rms_norm(o, out_norm, NA_HEADS)


def memory_group(q_in, mem, mem_norm, mem_wkv, q_norm, k_norm, out_norm):
    q = rms_norm(split_heads(q_in, MEM_HEADS), q_norm)
    kv = rms_norm(mem, mem_norm) @ mem_wkv
    k_m, v_m = jnp.split(kv, 2, axis=-1)
    k_m = rms_norm(split_heads(k_m, MEM_HEADS), k_norm)
    v_m = split_heads(v_m, MEM_HEADS)
    s = jnp.einsum('bhqd,bhkd->bhqk', q, k_m).astype(jnp.float32) * (MEM_HD ** -0.5)
    p = jax.nn.softmax(s, axis=-1).astype(v_m.dtype)
    o = merge_heads(jnp.einsum('bhqk,bhkd->bhqd', p, v_m))
    return head_rms_norm(o, out_norm, MEM_HEADS)


def setup_inputs(seed: int = 0) -> dict:
    key = jax.random.key(seed)
    ks = jax.random.split(key, 22)
    f32 = jnp.float32

    def nrm(k, shape, scale):
        return jax.random.normal(k, shape, f32) * scale

    def gain(k, shape):
        return 1.0 + 0.05 * jax.random.normal(k, shape, f32)

    out_scale = (2 * DEPTH) ** -0.5
    return {
        "x": nrm(ks[0], (BATCH, SEQ, D_MODEL), 1.0),
        "mem": nrm(ks[1], (BATCH, MEM_TOKENS, D_MODEL), 1.0),
        "attn_norm": gain(ks[2], (DEPTH, D_MODEL)),
        "w_in": nrm(ks[3], (DEPTH, D_MODEL, IN_COLS), D_MODEL ** -0.5),
        "gla_wg2_f": nrm(ks[4], (DEPTH, GLA_GATE_RANK, GLA_DK), GLA_GATE_RANK ** -0.5),
        "gla_bg_f": nrm(ks[5], (DEPTH, GLA_DK), 0.1),
        "gla_wg2_b": nrm(ks[6], (DEPTH, GLA_GATE_RANK, GLA_DK), GLA_GATE_RANK ** -0.5),
        "gla_bg_b": nrm(ks[7], (DEPTH, GLA_DK), 0.1),
        "gla_out_norm": gain(ks[8], (DEPTH, GLA_DV)),
        "na_q_norm": gain(ks[9], (DEPTH, NA_HD)),
        "na_k_norm": gain(ks[10], (DEPTH, NA_HD)),
        "na_rpb": nrm(ks[11], (DEPTH, NA_HEADS, 2 * NA_ROWS - 1, 2 * NA_COLS - 1), 0.1),
        "na_out_norm": gain(ks[12], (DEPTH, NA_WIDTH)),
        "mem_norm": gain(ks[13], (DEPTH, D_MODEL)),
        "mem_wkv": nrm(ks[14], (DEPTH, D_MODEL, 2 * MEM_WIDTH), D_MODEL ** -0.5),
        "mem_q_norm": gain(ks[15], (DEPTH, MEM_HD)),
        "mem_k_norm": gain(ks[16], (DEPTH, MEM_HD)),
        "mem_out_norm": gain(ks[17], (DEPTH, MEM_WIDTH)),
        "w_out": nrm(ks[18], (DEPTH, MIX_WIDTH, D_MODEL), MIX_WIDTH ** -0.5 * out_scale),
        "ffn_norm": gain(ks[19], (DEPTH, D_MODEL)),
        "ffn_w13": nrm(ks[20], (DEPTH, D_MODEL, 2 * D_FF), D_MODEL ** -0.5),
        "ffn_w2": nrm(ks[21], (DEPTH, D_FF, D_MODEL), D_FF ** -0.5 * out_scale),
    }


def reference(x, mem, attn_norm, w_in, gla_wg2_f, gla_bg_f, gla_wg2_b, gla_bg_b, gla_out_norm,
              na_q_norm, na_k_norm, na_rpb, na_out_norm, mem_norm, mem_wkv, mem_q_norm, mem_k_norm,
              mem_out_norm, w_out, ffn_norm, ffn_w13, ffn_w2):
    split_points = np.cumsum(IN_SPLITS)[:-1].tolist()
    for l in range(DEPTH):
        xn = rms_norm(x, attn_norm[l])
        proj = xn @ w_in[l]
        (g_q, g_k, g_v, g_r, g_f, g_b, n_q, n_k, n_v, m_q) = jnp.split(proj, split_points, axis=-1)
        y_gla = gla_group(g_q, g_k, g_v, g_r, g_f, g_b, gla_wg2_f[l], gla_bg_f[l],
                          gla_wg2_b[l], gla_bg_b[l], gla_out_norm[l])
        y_na = natten_group(n_q, n_k, n_v, na_q_norm[l], na_k_norm[l], na_rpb[l], na_out_norm[l])
        y_mem = memory_group(m_q, mem, mem_norm[l], mem_wkv[l], mem_q_norm[l], mem_k_norm[l],
                             mem_out_norm[l])
        x = x + jnp.concatenate([y_gla, y_na, y_mem], axis=-1) @ w_out[l]
        h = rms_norm(x, ffn_norm[l])
        gate, up = jnp.split(h @ ffn_w13[l], 2, axis=-1)
        x = x + (jax.nn.silu(gate) * up) @ ffn_w2[l]
    return x
```

```python
import functools

import numpy as np
import jax
import jax.numpy as jnp
from jax import lax
from jax.experimental import pallas as pl
from jax.experimental.pallas import tpu as pltpu

F32 = jnp.float32
BF16 = jnp.bfloat16

RMS_EPS = 1e-6
MASK_VALUE = -1e30

GRID_W = 64
GLA_HEADS = 4
GLA_HK = 128
GLA_HV = 256
GLA_RANK = 16
GLA_TAU = 16.0
GLA_CHUNK = 64
NA_HD = 64
NA_HEADS = 8
NA_ROWS = 8
NA_COLS = 16
MEM_HEADS = 4
MEM_HD = 128

V7X_VMEM_BYTES = 64 * 1024 * 1024
VMEM_LIMIT_BYTES = V7X_VMEM_BYTES - 8 * 1024 * 1024

NT_DIMS = (((1,), (1,)), ((), ()))


def _cparams(sem):
    return pltpu.CompilerParams(dimension_semantics=sem, vmem_limit_bytes=VMEM_LIMIT_BYTES)


def _rms(x, gain):
    ms = jnp.mean(x * x, axis=-1, keepdims=True)
    return (x * lax.rsqrt(ms + RMS_EPS)) * gain


def _in_proj_kernel(x_ref, g_ref, w_ref, wg_ref, proj_ref, gate_ref, xn_ref):
    @pl.when(pl.program_id(1) == 0)
    def _():
        xn = _rms(x_ref[...], g_ref[...]).astype(BF16)
        xn_ref[...] = xn
        gate_ref[...] = jnp.dot(xn, wg_ref[...], preferred_element_type=F32)

    proj_ref[...] = jnp.dot(xn_ref[...], w_ref[...], preferred_element_type=F32).astype(BF16)


def in_proj(x, gain, w_main, w_gate, *, tm, tn):
    T, D = x.shape
    N = w_main.shape[1]
    G = w_gate.shape[1]
    return pl.pallas_call(
        _in_proj_kernel,
        grid=(T // tm, N // tn),
        in_specs=[
            pl.BlockSpec((tm, D), lambda i, j: (i, 0)),
            pl.BlockSpec((1, D), lambda i, j: (0, 0)),
            pl.BlockSpec((D, tn), lambda i, j: (0, j)),
            pl.BlockSpec((D, G), lambda i, j: (0, 0)),
        ],
        out_specs=[
            pl.BlockSpec((tm, tn), lambda i, j: (i, j)),
            pl.BlockSpec((tm, G), lambda i, j: (i, 0)),
        ],
        out_shape=[jax.ShapeDtypeStruct((T, N), BF16), jax.ShapeDtypeStruct((T, G), F32)],
        scratch_shapes=[pltpu.VMEM((tm, D), BF16)],
        compiler_params=_cparams(("parallel", "arbitrary")),
        name="in_proj",
    )(x, gain, w_main, w_gate)


def _log_sigmoid(z):
    return jnp.minimum(z, 0.0) - jnp.log(1.0 + jnp.exp(-jnp.abs(z)))


def _gla_kernel(q_ref, k_ref, v_ref, gate_ref, wg_ref, bg_ref, *rest, reverse, final, gate_off):
    if final:
        r_ref, of_ref, gn_ref, o_ref, st_ref = rest
    else:
        o_ref, st_ref = rest
    C = GLA_CHUNK
    tb = q_ref.shape[0]
    nchunk = tb // C

    @pl.when(pl.program_id(2) == 0)
    def _():
        st_ref[...] = jnp.zeros_like(st_ref)

    g = gate_ref[:, gate_off:gate_off + GLA_RANK]
    z = jnp.dot(g.astype(BF16), wg_ref[...].astype(BF16), preferred_element_type=F32) + bg_ref[...]
    la = _log_sigmoid(z) * (1.0 / GLA_TAU)

    la_t = jnp.concatenate([la[c * C:(c + 1) * C, :] for c in range(nchunk)], axis=1)
    row = lax.broadcasted_iota(jnp.int32, (C, C), 0)
    col = lax.broadcasted_iota(jnp.int32, (C, C), 1)
    if reverse:
        tri = (col >= row)
        keep = col > row
    else:
        tri = (col <= row)
        keep = col <= row
    tri = jnp.where(tri, 1.0, 0.0).astype(BF16)
    la_hi = la_t.astype(BF16)
    la_lo = (la_t - la_hi.astype(F32)).astype(BF16)
    cum = (jnp.dot(tri, la_hi, preferred_element_type=F32)
           + jnp.dot(tri, la_lo, preferred_element_type=F32))

    dk = q_ref.shape[1]
    order = range(nchunk - 1, -1, -1) if reverse else range(nchunk)
    for c in order:
        rows = slice(c * C, (c + 1) * C)
        cum_c = cum[:, c * dk:(c + 1) * dk]
        last = cum_c[0:1, :] if reverse else cum_c[C - 1:C, :]
        qc = q_ref[rows, :].astype(F32) * (GLA_HK ** -0.5)
        kc = k_ref[rows, :].astype(F32)
        vc = v_ref[rows, :]
        q_e = (qc * jnp.exp(cum_c)).astype(BF16)
        k_e = (kc * jnp.exp(-cum_c)).astype(BF16)
        k_end = (kc * jnp.exp(last - cum_c)).astype(BF16)
        sc = lax.dot_general(q_e, k_e, NT_DIMS, preferred_element_type=F32)
        sc = jnp.where(keep, sc, 0.0).astype(BF16)
        st = st_ref[...]
        o = (jnp.dot(sc, vc, preferred_element_type=F32)
             + lax.dot_general(q_e, st.astype(BF16), NT_DIMS, preferred_element_type=F32))
        v_t = vc.astype(F32).T.astype(BF16)
        st_ref[...] = st * jnp.exp(last) + jnp.dot(v_t, k_end, preferred_element_type=F32)
        if final:
            o = o + of_ref[rows, :]
            y = _rms(o, gn_ref[...])
            r = r_ref[rows, :].astype(F32)
            o_ref[rows, :] = (y * (r * jax.nn.sigmoid(r))).astype(o_ref.dtype)
        else:
            o_ref[rows, :] = o


def gla_direction(proj, gates, wg2, bg, *, batch, tb, reverse, o_fwd=None, out_gain=None):
    T = proj.shape[0]
    S = T // batch
    nb = S // tb
    final = o_fwd is not None
    H, dk, dv = GLA_HEADS, GLA_HK, GLA_HV

    def tok(b, h, i):
        return b * nb + ((nb - 1 - i) if reverse else i)

    in_specs = [
        pl.BlockSpec((tb, dk), lambda b, h, i: (tok(b, h, i), h)),
        pl.BlockSpec((tb, dk), lambda b, h, i: (tok(b, h, i), H + h)),
        pl.BlockSpec((tb, dv), lambda b, h, i: (tok(b, h, i), (2 * H * dk) // dv + h)),
        pl.BlockSpec((tb, gates.shape[1]), lambda b, h, i: (tok(b, h, i), 0)),
        pl.BlockSpec((GLA_RANK, dk), lambda b, h, i: (0, h)),
        pl.BlockSpec((1, dk), lambda b, h, i: (0, h)),
    ]
    args = [proj, proj, proj, gates, wg2, bg]
    if final:
        in_specs += [
            pl.BlockSpec((tb, dv), lambda b, h, i: (tok(b, h, i), (2 * H * dk) // dv + H + h)),
            pl.BlockSpec((tb, dv), lambda b, h, i: (tok(b, h, i), h)),
            pl.BlockSpec((1, dv), lambda b, h, i: (0, h)),
        ]
        args += [proj, o_fwd, out_gain]
    kern = functools.partial(_gla_kernel, reverse=reverse, final=final,
                             gate_off=GLA_RANK if reverse else 0)
    return pl.pallas_call(
        kern,
        grid=(batch, H, nb),
        in_specs=in_specs,
        out_specs=pl.BlockSpec((tb, dv), lambda b, h, i: (tok(b, h, i), h)),
        out_shape=jax.ShapeDtypeStruct((T, H * dv), BF16 if final else F32),
        scratch_shapes=[pltpu.VMEM((dv, dk), F32)],
        compiler_params=_cparams(("parallel", "parallel", "arbitrary")),
        name="gla_bwd" if reverse else "gla_fwd",
    )(*args)


NA_QROWS = 4
NA_KSLOTS = 3 * NA_QROWS


def _head_pair_sums(x):
    r = lax.broadcasted_iota(jnp.int32, (2 * NA_HD, 2 * NA_HD), 0) // NA_HD
    c = lax.broadcasted_iota(jnp.int32, (2 * NA_HD, 2 * NA_HD), 1) // NA_HD
    ones_bd = jnp.where(r == c, 1.0, 0.0).astype(BF16)
    hi = x.astype(BF16)
    lo = (x - hi.astype(F32)).astype(BF16)
    return (jnp.dot(hi, ones_bd, preferred_element_type=F32)
            + jnp.dot(lo, ones_bd, preferred_element_type=F32))


def _head_pair_rms(x, gain):
    ms = _head_pair_sums(x * x) * (1.0 / NA_HD)
    return (x * lax.rsqrt(ms + RMS_EPS)) * gain


def _natten_kernel(q_ref, kp_ref, kc_ref, kn_ref, vp_ref, vc_ref, vn_ref, bias_ref,
                   gq_ref, gk_ref, go_ref, o_ref):
    nq = q_ref.shape[0]
    nk = 3 * nq
    lane = lax.broadcasted_iota(jnp.int32, (1, 2 * NA_HD), 1)
    head0 = lane < NA_HD

    q = _head_pair_rms(q_ref[...].astype(F32), gq_ref[...]) * (NA_HD ** -0.5)
    k = jnp.concatenate([kp_ref[...], kc_ref[...], kn_ref[...]], axis=0).astype(F32)
    k = _head_pair_rms(k, gk_ref[...])
    v = jnp.concatenate([vp_ref[...], vc_ref[...], vn_ref[...]], axis=0)
    k_bd = jnp.concatenate([jnp.where(head0, k, 0.0), jnp.where(head0, 0.0, k)], axis=0).astype(BF16)
    zero = jnp.zeros_like(v)
    v_bd = jnp.concatenate([jnp.where(head0, v, zero), jnp.where(head0, zero, v)], axis=0)

    s = lax.dot_general(q.astype(BF16), k_bd, NT_DIMS, preferred_element_type=F32)
    s = s + bias_ref[0, 0]
    s0 = s[:, :nk]
    s1 = s[:, nk:]
    p0 = jnp.exp(s0 - jnp.max(s0, axis=-1, keepdims=True))
    p1 = jnp.exp(s1 - jnp.max(s1, axis=-1, keepdims=True))
    l0 = jnp.sum(p0, axis=-1, keepdims=True)
    l1 = jnp.sum(p1, axis=-1, keepdims=True)
    p = jnp.concatenate([p0, p1], axis=1).astype(BF16)
    o = jnp.dot(p, v_bd, preferred_element_type=F32)
    o = o / jnp.where(head0, l0, l1)
    o_ref[...] = _head_pair_rms(o, go_ref[...]).astype(o_ref.dtype)


def _natten_bias_indices(rows):
    m, W = NA_QROWS, GRID_W
    kr_win = min(NA_ROWS, rows)
    nblk = rows // m
    dr = np.full((3, m, NA_KSLOTS), -1, np.int64)
    for var, blk in enumerate((0, 1, nblk - 1)):
        r0 = blk * m
        for a in range(m):
            r = r0 + a
            rs = int(np.clip(r - kr_win // 2, 0, rows - kr_win))
            for slot in range(NA_KSLOTS):
                src_blk = blk - 1 + slot // m
                if src_blk < 0 or src_blk >= nblk:
                    continue
                kr = src_blk * m + slot % m
                if rs <= kr < rs + kr_win:
                    dr[var, a, slot] = kr - r + (NA_ROWS - 1)
    qc = np.arange(W)[:, None]
    kc = np.arange(W)[None, :]
    dc = np.clip(kc - qc, -(NA_COLS - 1), NA_COLS - 1) + (NA_COLS - 1)
    cs = np.clip(qc - NA_COLS // 2, 0, W - NA_COLS)
    colmask = (kc >= cs) & (kc < cs + NA_COLS)
    return dr, dc, colmask


def natten_bias_table(rpb, rows):
    H = rpb.shape[0]
    m, W = NA_QROWS, GRID_W
    dr, dc, colmask = _natten_bias_indices(rows)
    valid = (dr >= 0)[:, :, None, :, None] & colmask[None, None, :, None, :]
    dr_c = np.maximum(dr, 0)
    b = rpb[:, dr_c[:, :, None, :, None], dc[None, None, :, None, :]]
    b = jnp.where(valid[None], b.astype(F32), MASK_VALUE)
    b = b.reshape(H // 2, 2, 3, m * W, NA_KSLOTS * W)
    b = b.transpose(2, 0, 3, 1, 4).reshape(3, H // 2, m * W, 2 * NA_KSLOTS * W)
    return b


def natten(proj, bias, gq, gk, go, *, batch, col0):
    T = proj.shape[0]
    S = T // batch
    nq = NA_QROWS * GRID_W
    nblk = S // nq
    npair = NA_HEADS // 2
    lw = 2 * NA_HD
    cq, ck, cv = col0 // lw, col0 // lw + npair, col0 // lw + 2 * npair

    def prev(i):
        return jnp.maximum(i - 1, 0)

    def nxt(i):
        return jnp.minimum(i + 1, nblk - 1)

    def variant(i):
        return jnp.where(i == 0, 0, jnp.where(i == nblk - 1, 2, 1))

    in_specs = [
        pl.BlockSpec((nq, lw), lambda hp, b, i: (b * nblk + i, cq + hp)),
        pl.BlockSpec((nq, lw), lambda hp, b, i: (b * nblk + prev(i), ck + hp)),
        pl.BlockSpec((nq, lw), lambda hp, b, i: (b * nblk + i, ck + hp)),
        pl.BlockSpec((nq, lw), lambda hp, b, i: (b * nblk + nxt(i), ck + hp)),
        pl.BlockSpec((nq, lw), lambda hp, b, i: (b * nblk + prev(i), cv + hp)),
        pl.BlockSpec((nq, lw), lambda hp, b, i: (b * nblk + i, cv + hp)),
        pl.BlockSpec((nq, lw), lambda hp, b, i: (b * nblk + nxt(i), cv + hp)),
        pl.BlockSpec((1, 1, nq, 2 * NA_KSLOTS * GRID_W), lambda hp, b, i: (variant(i), hp, 0, 0)),
        pl.BlockSpec((1, lw), lambda hp, b, i: (0, 0)),
        pl.BlockSpec((1, lw), lambda hp, b, i: (0, 0)),
        pl.BlockSpec((1, lw), lambda hp, b, i: (0, hp)),
    ]
    return pl.pallas_call(
        _natten_kernel,
        grid=(npair, batch, nblk),
        in_specs=in_specs,
        out_specs=pl.BlockSpec((nq, lw), lambda hp, b, i: (b * nblk + i, hp)),
        out_shape=jax.ShapeDtypeStruct((T, NA_HEADS * NA_HD), BF16),
        compiler_params=_cparams(("parallel", "parallel", "arbitrary")),
        name="natten",
    )(proj, proj, proj, proj, proj, proj, proj, bias, gq, gk, go)


def _mem_kv_kernel(mem_ref, g_ref, w_ref, gk_ref, k_ref, v_ref):
    mn = _rms(mem_ref[...], g_ref[...]).astype(BF16)
    kv = jnp.dot(mn, w_ref[...], preferred_element_type=F32)
    width = k_ref.shape[1]
    for h in range(MEM_HEADS):
        cols = slice(h * MEM_HD, (h + 1) * MEM_HD)
        k_ref[:, cols] = _rms(kv[:, cols], gk_ref[...]).astype(BF16)
    v_ref[...] = kv[:, width:].astype(BF16)


def mem_kv(mem2d, gain, wkv, gk):
    n, D = mem2d.shape
    width = wkv.shape[1] // 2
    return pl.pallas_call(
        _mem_kv_kernel,
        out_shape=[jax.ShapeDtypeStruct((n, width), BF16), jax.ShapeDtypeStruct((n, width), BF16)],
        compiler_params=pltpu.CompilerParams(vmem_limit_bytes=VMEM_LIMIT_BYTES),
        name="mem_kv",
    )(mem2d, gain, wkv, gk)


def _mem_attn_kernel(q_ref, k_ref, v_ref, gq_ref, go_ref, o_ref):
    for h in range(MEM_HEADS):
        cols = slice(h * MEM_HD, (h + 1) * MEM_HD)
        q = _rms(q_ref[:, cols].astype(F32), gq_ref[...]) * (MEM_HD ** -0.5)
        s = lax.dot_general(q.astype(BF16), k_ref[:, cols], NT_DIMS, preferred_element_type=F32)
        p = jnp.exp(s - jnp.max(s, axis=-1, keepdims=True))
        l = jnp.sum(p, axis=-1, keepdims=True)
        o = jnp.dot(p.astype(BF16), v_ref[:, cols], preferred_element_type=F32) / l
        o_ref[:, cols] = _rms(o, go_ref[:, cols]).astype(o_ref.dtype)


def mem_attn(proj, k_m, v_m, gq, go, *, batch, tm, col0):
    T = proj.shape[0]
    S = T // batch
    nm = k_m.shape[0] // batch
    width = MEM_HEADS * MEM_HD
    per_b = S // tm
    return pl.pallas_call(
        _mem_attn_kernel,
        grid=(T // tm,),
        in_specs=[
            pl.BlockSpec((tm, width), lambda i: (i, col0 // width)),
            pl.BlockSpec((nm, width), lambda i: (i // per_b, 0)),
            pl.BlockSpec((nm, width), lambda i: (i // per_b, 0)),
            pl.BlockSpec((1, MEM_HD), lambda i: (0, 0)),
            pl.BlockSpec((1, width), lambda i: (0, 0)),
        ],
        out_specs=pl.BlockSpec((tm, width), lambda i: (i, 0)),
        out_shape=jax.ShapeDtypeStruct((T, width), BF16),
        compiler_params=_cparams(("parallel",)),
        name="mem_attn",
    )(proj, k_m, v_m, gq, go)


def _out_proj_kernel(x_ref, ya_ref, yb_ref, yc_ref, wa_ref, wb_ref, wc_ref, o_ref):
    acc = jnp.dot(ya_ref[...], wa_ref[...], preferred_element_type=F32)
    acc += jnp.dot(yb_ref[...], wb_ref[...], preferred_element_type=F32)
    acc += jnp.dot(yc_ref[...], wc_ref[...], preferred_element_type=F32)
    o_ref[...] = x_ref[...] + acc


def out_proj(x, y_gla, y_na, y_mem, w_out, *, tm, tn):
    T, D = x.shape
    wa, wb, wc = y_gla.shape[1], y_na.shape[1], y_mem.shape[1]
    assert wa % wb == 0 and wb == wc
    return pl.pallas_call(
        _out_proj_kernel,
        grid=(T // tm, D // tn),
        in_specs=[
            pl.BlockSpec((tm, tn), lambda i, j: (i, j)),
            pl.BlockSpec((tm, wa), lambda i, j: (i, 0)),
            pl.BlockSpec((tm, wb), lambda i, j: (i, 0)),
            pl.BlockSpec((tm, wc), lambda i, j: (i, 0)),
            pl.BlockSpec((wa, tn), lambda i, j: (0, j)),
            pl.BlockSpec((wb, tn), lambda i, j: (wa // wb, j)),
            pl.BlockSpec((wc, tn), lambda i, j: (wa // wb + 1, j)),
        ],
        out_specs=pl.BlockSpec((tm, tn), lambda i, j: (i, j)),
        out_shape=jax.ShapeDtypeStruct((T, D), F32),
        compiler_params=_cparams(("parallel", "arbitrary")),
        name="out_proj",
    )(x, y_gla, y_na, y_mem, w_out, w_out, w_out)


def _ffn_kernel(x_ref, g_ref, w1_ref, w3_ref, w2_ref, o_ref, h_ref):
    @pl.when(pl.program_id(1) == 0)
    def _():
        x = x_ref[...]
        h_ref[...] = _rms(x, g_ref[...]).astype(BF16)
        o_ref[...] = x

    h = h_ref[...]
    gate = jnp.dot(h, w1_ref[...], preferred_element_type=F32)
    up = jnp.dot(h, w3_ref[...], preferred_element_type=F32)
    act = (gate * jax.nn.sigmoid(gate) * up).astype(BF16)
    o_ref[...] += jnp.dot(act, w2_ref[...], preferred_element_type=F32)


def ffn(x, gain, w13, w2, *, tm, tf):
    T, D = x.shape
    F = w2.shape[0]
    nf = F // tf
    return pl.pallas_call(
        _ffn_kernel,
        grid=(T // tm, nf),
        in_specs=[
            pl.BlockSpec((tm, D), lambda i, f: (i, 0)),
            pl.BlockSpec((1, D), lambda i, f: (0, 0)),
            pl.BlockSpec((D, tf), lambda i, f: (0, f)),
            pl.BlockSpec((D, tf), lambda i, f: (0, nf + f)),
            pl.BlockSpec((tf, D), lambda i, f: (f, 0)),
        ],
        out_specs=pl.BlockSpec((tm, D), lambda i, f: (i, 0)),
        out_shape=jax.ShapeDtypeStruct((T, D), F32),
        scratch_shapes=[pltpu.VMEM((tm, D), BF16)],
        compiler_params=_cparams(("parallel", "arbitrary")),
        name="ffn",
    )(x, gain, w13, w13, w2)


def kernel(x, mem, attn_norm, w_in, gla_wg2_f, gla_bg_f, gla_wg2_b, gla_bg_b, gla_out_norm,
           na_q_norm, na_k_norm, na_rpb, na_out_norm, mem_norm, mem_wkv, mem_q_norm, mem_k_norm,
           mem_out_norm, w_out, ffn_norm, ffn_w13, ffn_w2):
    B, S, D = x.shape
    depth = w_in.shape[0]
    T = B * S
    rows = S // GRID_W
    gla_dk = GLA_HEADS * GLA_HK
    gla_dv = GLA_HEADS * GLA_HV
    na_w = NA_HEADS * NA_HD
    mem_w = MEM_HEADS * MEM_HD
    gate0 = 2 * gla_dk + 2 * gla_dv
    gate1 = gate0 + 2 * GLA_RANK
    n_main = w_in.shape[2] - 2 * GLA_RANK
    na_col0 = gate0
    mem_col0 = gate0 + 3 * na_w

    w_main = jnp.concatenate([w_in[:, :, :gate0], w_in[:, :, gate1:]], axis=2).astype(BF16)
    w_gate = jnp.pad(w_in[:, :, gate0:gate1], ((0, 0), (0, 0), (0, 128 - 2 * GLA_RANK))).astype(BF16)
    w_out_b = w_out.astype(BF16)
    w13_b = ffn_w13.astype(BF16)
    w2_b = ffn_w2.astype(BF16)
    wkv_b = mem_wkv.astype(BF16)
    na_gq = jnp.tile(na_q_norm, (1, 2))[:, None, :]
    na_gk = jnp.tile(na_k_norm, (1, 2))[:, None, :]

    xf = x.reshape(T, D)
    mem2d = mem.reshape(B * mem.shape[1], D)
    for l in range(depth):
        proj, gates = in_proj(xf, attn_norm[l][None], w_main[l], w_gate[l], tm=1024, tn=1024)
        o_f = gla_direction(proj, gates, gla_wg2_f[l], gla_bg_f[l][None], batch=B, tb=512, reverse=False)
        y_gla = gla_direction(proj, gates, gla_wg2_b[l], gla_bg_b[l][None], batch=B, tb=512, reverse=True,
                              o_fwd=o_f, out_gain=gla_out_norm[l][None])
        bias = natten_bias_table(na_rpb[l], rows)
        y_na = natten(proj, bias, na_gq[l], na_gk[l], na_out_norm[l][None], batch=B, col0=na_col0)
        k_m, v_m = mem_kv(mem2d, mem_norm[l][None], wkv_b[l], mem_k_norm[l][None])
        y_mem = mem_attn(proj, k_m, v_m, mem_q_norm[l][None], mem_out_norm[l][None],
                         batch=B, tm=512, col0=mem_col0)
        xf = out_proj(xf, y_gla, y_na, y_mem, w_out_b[l], tm=1024, tn=1024)
        xf = ffn(xf, ffn_norm[l][None], w13_b[l], w2_b[l], tm=512, tf=512)
    assert n_main == mem_col0 + mem_w and gla_dv + na_w + mem_w == w_out.shape[1]
    return xf.reshape(B, S, D)
```

```python
import functools

import jax
import jax.numpy as jnp
from jax import lax
from jax.experimental import pallas as pl
from jax.experimental.pallas import tpu as pltpu

F32 = jnp.float32
BF16 = jnp.bfloat16

RMS_EPS = 1e-6
MASK_VALUE = -1e30

GRID_W = 64
GLA_HEADS = 4
GLA_HK = 128
GLA_HV = 256
GLA_RANK = 16
GLA_TAU = 16.0
GLA_CHUNK = 64
NA_HD = 64
NA_HEADS = 8
NA_ROWS = 8
NA_COLS = 16
MEM_HEADS = 4
MEM_HD = 128

V7X_VMEM_BYTES = 64 * 1024 * 1024
VMEM_LIMIT_BYTES = V7X_VMEM_BYTES - 8 * 1024 * 1024

NT_DIMS = (((1,), (1,)), ((), ()))


def _cparams(sem):
    return pltpu.CompilerParams(dimension_semantics=sem, vmem_limit_bytes=VMEM_LIMIT_BYTES)


def _rms(x, gain):
    ms = jnp.mean(x * x, axis=-1, keepdims=True)
    return (x * lax.rsqrt(ms + RMS_EPS)) * gain


def _head_pair_ones():
    r = lax.broadcasted_iota(jnp.int32, (2 * NA_HD, 2 * NA_HD), 0) // NA_HD
    c = lax.broadcasted_iota(jnp.int32, (2 * NA_HD, 2 * NA_HD), 1) // NA_HD
    return jnp.where(r == c, 1.0, 0.0).astype(BF16)


def _head_pair_rms(x, gain, ones_bd):
    ms = jnp.dot((x * x).astype(BF16), ones_bd, preferred_element_type=F32) * (1.0 / NA_HD)
    return (x * lax.rsqrt(ms + RMS_EPS)) * gain


def _in_proj_kernel(x_ref, g_ref, w_ref, wg_ref, gqk_ref, proj_ref, gate_ref, xn_ref, *, qk_tile):
    j = pl.program_id(1)

    @pl.when(j == 0)
    def _():
        xn = _rms(x_ref[...], g_ref[...]).astype(BF16)
        xn_ref[...] = xn
        gate_ref[...] = jnp.dot(xn, wg_ref[...], preferred_element_type=F32)

    acc = jnp.dot(xn_ref[...], w_ref[...], preferred_element_type=F32)

    @pl.when(j != qk_tile)
    def _():
        proj_ref[...] = acc.astype(BF16)

    @pl.when(j == qk_tile)
    def _():
        ones_bd = _head_pair_ones()
        lw = 2 * NA_HD
        for c in range(acc.shape[1] // lw):
            cols = slice(c * lw, (c + 1) * lw)
            proj_ref[:, cols] = _head_pair_rms(acc[:, cols], gqk_ref[:, cols], ones_bd).astype(BF16)


def in_proj(x, gain, w_main, w_gate, gqk, *, layer, tm, tn, qk_tile):
    T, D = x.shape
    N = w_main.shape[2]
    G = w_gate.shape[2]
    return pl.pallas_call(
        functools.partial(_in_proj_kernel, qk_tile=qk_tile),
        grid=(T // tm, N // tn),
        in_specs=[
            pl.BlockSpec((tm, D), lambda i, j: (i, 0)),
            pl.BlockSpec((1, D), lambda i, j: (0, 0)),
            pl.BlockSpec((None, D, tn), lambda i, j: (layer, 0, j)),
            pl.BlockSpec((None, D, G), lambda i, j: (layer, 0, 0)),
            pl.BlockSpec((1, tn), lambda i, j: (0, 0)),
        ],
        out_specs=[
            pl.BlockSpec((tm, tn), lambda i, j: (i, j)),
            pl.BlockSpec((tm, G), lambda i, j: (i, 0)),
        ],
        out_shape=[jax.ShapeDtypeStruct((T, N), BF16), jax.ShapeDtypeStruct((T, G), F32)],
        scratch_shapes=[pltpu.VMEM((tm, D), BF16)],
        compiler_params=_cparams(("parallel", "arbitrary")),
        name="in_proj",
    )(x, gain, w_main, w_gate, gqk)


def _log_sigmoid(z):
    return jnp.minimum(z, 0.0) - jnp.log(1.0 + jnp.exp(-jnp.abs(z)))


def _gla_kernel(q_ref, k_ref, v_ref, gate_ref, wg_ref, bg_ref, *rest, reverse, final, gate_off):
    if final:
        r_ref, of_ref, gn_ref, o_ref, st_ref = rest
    else:
        o_ref, st_ref = rest
    C = GLA_CHUNK
    H, dk, dv = GLA_HEADS, GLA_HK, GLA_HV
    tb = q_ref.shape[0]
    nchunk = tb // C

    @pl.when(pl.program_id(1) == 0)
    def _():
        st_ref[...] = jnp.zeros_like(st_ref)

    g = gate_ref[:, gate_off:gate_off + GLA_RANK]
    z = jnp.dot(g.astype(BF16), wg_ref[...].astype(BF16), preferred_element_type=F32) + bg_ref[...]
    la = _log_sigmoid(z) * (1.0 / GLA_TAU)

    la_t = jnp.concatenate([la[c * C:(c + 1) * C, :] for c in range(nchunk)], axis=1)
    row = lax.broadcasted_iota(jnp.int32, (C, C), 0)
    col = lax.broadcasted_iota(jnp.int32, (C, C), 1)
    if reverse:
        tri = (col >= row)
        keep = col > row
    else:
        tri = (col <= row)
        keep = col <= row
    tri = jnp.where(tri, 1.0, 0.0).astype(BF16)
    la_hi = la_t.astype(BF16)
    la_lo = (la_t - la_hi.astype(F32)).astype(BF16)
    cum = (jnp.dot(tri, la_hi, preferred_element_type=F32)
           + jnp.dot(tri, la_lo, preferred_element_type=F32))

    order = range(nchunk - 1, -1, -1) if reverse else range(nchunk)
    for c in order:
        rows = slice(c * C, (c + 1) * C)
        for h in range(H):
            kcols = slice(h * dk, (h + 1) * dk)
            vcols = slice(h * dv, (h + 1) * dv)
            cum_c = cum[:, (c * H + h) * dk:(c * H + h + 1) * dk]
            last = cum_c[0:1, :] if reverse else cum_c[C - 1:C, :]
            qc = q_ref[rows, kcols].astype(F32) * (GLA_HK ** -0.5)
            kc = k_ref[rows, kcols].astype(F32)
            vc = v_ref[rows, vcols]
            q_e = (qc * jnp.exp(cum_c)).astype(BF16)
            k_e = (kc * jnp.exp(-cum_c)).astype(BF16)
            k_end = (kc * jnp.exp(last - cum_c)).astype(BF16)
            sc = lax.dot_general(q_e, k_e, NT_DIMS, preferred_element_type=F32)
            sc = jnp.where(keep, sc, 0.0).astype(BF16)
            st = st_ref[h]
            o = (jnp.dot(sc, vc, preferred_element_type=F32)
                 + lax.dot_general(q_e, st.astype(BF16), NT_DIMS, preferred_element_type=F32))
            v_t = vc.astype(F32).T.astype(BF16)
            st_ref[h] = st * jnp.exp(last) + jnp.dot(v_t, k_end, preferred_element_type=F32)
            if final:
                o = o + of_ref[rows, vcols]
                y = _rms(o, gn_ref[:, vcols])
                r = r_ref[rows, vcols].astype(F32)
                o_ref[rows, vcols] = (y * (r * jax.nn.sigmoid(r))).astype(o_ref.dtype)
            else:
                o_ref[rows, vcols] = o


def gla_direction(proj, gates, wg2, bg, *, batch, tb, reverse, o_fwd=None, out_gain=None):
    T = proj.shape[0]
    S = T // batch
    nb = S // tb
    final = o_fwd is not None
    H, dk, dv = GLA_HEADS, GLA_HK, GLA_HV
    wk, wv = H * dk, H * dv

    def tok(b, i):
        return b * nb + ((nb - 1 - i) if reverse else i)

    in_specs = [
        pl.BlockSpec((tb, wk), lambda b, i: (tok(b, i), 0)),
        pl.BlockSpec((tb, wk), lambda b, i: (tok(b, i), 1)),
        pl.BlockSpec((tb, wv), lambda b, i: (tok(b, i), (2 * wk) // wv)),
        pl.BlockSpec((tb, gates.shape[1]), lambda b, i: (tok(b, i), 0)),
        pl.BlockSpec((GLA_RANK, wk), lambda b, i: (0, 0)),
        pl.BlockSpec((1, wk), lambda b, i: (0, 0)),
    ]
    args = [proj, proj, proj, gates, wg2, bg]
    if final:
        in_specs += [
            pl.BlockSpec((tb, wv), lambda b, i: (tok(b, i), (2 * wk) // wv + 1)),
            pl.BlockSpec((tb, wv), lambda b, i: (tok(b, i), 0)),
            pl.BlockSpec((1, wv), lambda b, i: (0, 0)),
        ]
        args += [proj, o_fwd, out_gain]
    kern = functools.partial(_gla_kernel, reverse=reverse, final=final,
                             gate_off=GLA_RANK if reverse else 0)
    return pl.pallas_call(
        kern,
        grid=(batch, nb),
        in_specs=in_specs,
        out_specs=pl.BlockSpec((tb, wv), lambda b, i: (tok(b, i), 0)),
        out_shape=jax.ShapeDtypeStruct((T, wv), BF16 if final else F32),
        scratch_shapes=[pltpu.VMEM((H, dv, dk), F32)],
        compiler_params=_cparams(("parallel", "arbitrary")),
        name="gla_bwd" if reverse else "gla_fwd",
    )(*args)


NA_QROWS = 4
NA_DR = 2 * NA_ROWS - 1
NA_DC = 2 * NA_COLS - 1


def _natten_bias_kernel(rpb_ref, o_ref):
    W = GRID_W
    base = (pl.program_id(0) * NA_HEADS + pl.program_id(1)) * (NA_DR * NA_DC)
    qc = lax.broadcasted_iota(jnp.int32, (W, 2 * W), 0)
    ln = lax.broadcasted_iota(jnp.int32, (W, 2 * W), 1)
    kc = ln & (W - 1)
    dc_idx = jnp.clip(kc - qc, -(NA_COLS - 1), NA_COLS - 1) + (NA_COLS - 1)
    cs = jnp.clip(qc - NA_COLS // 2, 0, W - NA_COLS)
    in_win = (kc >= cs) & (kc < cs + NA_COLS)
    left = ln < W
    toeplitz = []
    for dr in range(NA_DR):
        t = jnp.full((W, 2 * W), MASK_VALUE, F32)
        for dc in range(NA_DC):
            t = jnp.where(dc_idx == dc, rpb_ref[base + dr * NA_DC + dc], t)
        toeplitz.append(jnp.where(in_win, t, MASK_VALUE))
    for dr0 in range(NA_ROWS):
        for jp in range(NA_ROWS // 2):
            o_ref[dr0, :, jp * 2 * W:(jp + 1) * 2 * W] = jnp.where(
                left, toeplitz[dr0 + 2 * jp], toeplitz[dr0 + 2 * jp + 1])


def natten_bias_tables(rpb):
    L, H = rpb.shape[:2]
    W = GRID_W
    return pl.pallas_call(
        _natten_bias_kernel,
        grid=(L, H),
        in_specs=[pl.BlockSpec(memory_space=pltpu.SMEM)],
        out_specs=pl.BlockSpec((None, None, NA_ROWS, W, NA_ROWS * W), lambda l, h: (l, h, 0, 0, 0)),
        out_shape=jax.ShapeDtypeStruct((L, H, NA_ROWS, W, NA_ROWS * W), F32),
        compiler_params=_cparams(("parallel", "parallel")),
        name="natten_bias",
    )(rpb.reshape(-1))


def _natten_kernel(q_ref, kp_ref, kc_ref, kn_ref, vp_ref, vc_ref, vn_ref, wb_ref, go_ref, o_ref,
                   k_all, v_all, o_scr, *, rows):
    W = GRID_W
    nq = q_ref.shape[0]
    lw = 2 * NA_HD
    nwin = NA_ROWS * W
    k_all[0:nq] = kp_ref[...]
    k_all[nq:2 * nq] = kc_ref[...]
    k_all[2 * nq:3 * nq] = kn_ref[...]
    v_all[0:nq] = vp_ref[...]
    v_all[nq:2 * nq] = vc_ref[...]
    v_all[2 * nq:3 * nq] = vn_ref[...]

    head0 = lax.broadcasted_iota(jnp.int32, (1, lw), 1) < NA_HD
    ones_bd = _head_pair_ones()
    blk = pl.program_id(1)
    nblk = rows // NA_QROWS

    def step(r0):
        for a in range(NA_QROWS):
            rs = min(max(r0 + a - NA_ROWS // 2, 0), rows - NA_ROWS)
            off = rs - (r0 - NA_QROWS)
            dr0 = rs - (r0 + a) + (NA_ROWS - 1)
            krows = slice(off * W, off * W + nwin)
            qrows = slice(a * W, (a + 1) * W)
            pairs = range(NA_HEADS // 2)
            cols = [slice(p * lw, (p + 1) * lw) for p in pairs]
            s = []
            for p in pairs:
                qa = q_ref[qrows, cols[p]]
                zero = jnp.zeros_like(qa)
                lhs = jnp.concatenate([jnp.where(head0, qa, zero), jnp.where(head0, zero, qa)], axis=0)
                sp = lax.dot_general(lhs, k_all[krows, cols[p]], NT_DIMS,
                                     preferred_element_type=F32)
                s.append(sp + jnp.concatenate([wb_ref[2 * p, dr0], wb_ref[2 * p + 1, dr0]], axis=0))
            m = [jnp.max(sp, axis=-1, keepdims=True) for sp in s]
            e = [jnp.exp(sp - mp) for sp, mp in zip(s, m)]
            l = [jnp.sum(ep, axis=-1, keepdims=True) for ep in e]
            r = [jnp.dot(e[p].astype(BF16), v_all[krows, cols[p]], preferred_element_type=F32) for p in pairs]
            for p in pairs:
                rp = r[p] / l[p]
                o_scr[qrows, cols[p]] = jnp.where(head0, rp[0:W], rp[W:2 * W])

    pl.when(blk == 0)(lambda: step(0))
    pl.when(jnp.logical_and(blk > 0, blk < nblk - 1))(lambda: step(NA_QROWS))
    pl.when(blk == nblk - 1)(lambda: step(rows - NA_QROWS))

    for p in range(NA_HEADS // 2):
        cols = slice(p * lw, (p + 1) * lw)
        o_ref[:, cols] = _head_pair_rms(o_scr[:, cols], go_ref[:, cols], ones_bd).astype(o_ref.dtype)


def natten(proj, wb, go, *, layer, batch, col0):
    T = proj.shape[0]
    S = T // batch
    rows = S // GRID_W
    nq = NA_QROWS * GRID_W
    nblk = S // nq
    width = NA_HEADS * NA_HD
    cq = col0 // width

    def prev(i):
        return jnp.maximum(i - 1, 0)

    def nxt(i):
        return jnp.minimum(i + 1, nblk - 1)

    in_specs = [
        pl.BlockSpec((nq, width), lambda b, i: (b * nblk + i, cq)),
        pl.BlockSpec((nq, width), lambda b, i: (b * nblk + prev(i), cq + 1)),
        pl.BlockSpec((nq, width), lambda b, i: (b * nblk + i, cq + 1)),
        pl.BlockSpec((nq, width), lambda b, i: (b * nblk + nxt(i), cq + 1)),
        pl.BlockSpec((nq, width), lambda b, i: (b * nblk + prev(i), cq + 2)),
        pl.BlockSpec((nq, width), lambda b, i: (b * nblk + i, cq + 2)),
        pl.BlockSpec((nq, width), lambda b, i: (b * nblk + nxt(i), cq + 2)),
        pl.BlockSpec((None,) + wb.shape[1:], lambda b, i: (layer, 0, 0, 0, 0)),
        pl.BlockSpec((1, width), lambda b, i: (0, 0)),
    ]
    return pl.pallas_call(
        functools.partial(_natten_kernel, rows=rows),
        grid=(batch, nblk),
        in_specs=in_specs,
        out_specs=pl.BlockSpec((nq, width), lambda b, i: (b * nblk + i, 0)),
        out_shape=jax.ShapeDtypeStruct((T, width), BF16),
        scratch_shapes=[pltpu.VMEM((3 * nq, width), BF16), pltpu.VMEM((3 * nq, width), BF16),
                        pltpu.VMEM((nq, width), F32)],
        compiler_params=_cparams(("parallel", "arbitrary")),
        name="natten",
    )(proj, proj, proj, proj, proj, proj, proj, wb, go)


def _mem_kv_kernel(mem_ref, g_ref, w_ref, gk_ref, k_ref, v_ref):
    mn = _rms(mem_ref[...], g_ref[...]).astype(BF16)
    kv = jnp.dot(mn, w_ref[...], preferred_element_type=F32)
    width = k_ref.shape[1]
    for h in range(MEM_HEADS):
        cols = slice(h * MEM_HD, (h + 1) * MEM_HD)
        k_ref[:, cols] = _rms(kv[:, cols], gk_ref[...]).astype(BF16)
    v_ref[...] = kv[:, width:].astype(BF16)


def mem_kv(mem2d, gain, wkv, gk, *, layer):
    n, D = mem2d.shape
    width = wkv.shape[2] // 2
    return pl.pallas_call(
        _mem_kv_kernel,
        grid=(1,),
        in_specs=[
            pl.BlockSpec((n, D), lambda i: (0, 0)),
            pl.BlockSpec((1, D), lambda i: (0, 0)),
            pl.BlockSpec((None, D, 2 * width), lambda i: (layer, 0, 0)),
            pl.BlockSpec((1, MEM_HD), lambda i: (0, 0)),
        ],
        out_specs=[pl.BlockSpec((n, width), lambda i: (0, 0)), pl.BlockSpec((n, width), lambda i: (0, 0))],
        out_shape=[jax.ShapeDtypeStruct((n, width), BF16), jax.ShapeDtypeStruct((n, width), BF16)],
        compiler_params=_cparams(("arbitrary",)),
        name="mem_kv",
    )(mem2d, gain, wkv, gk)


def _mem_attn_kernel(q_ref, k_ref, v_ref, gq_ref, go_ref, o_ref):
    for h in range(MEM_HEADS):
        cols = slice(h * MEM_HD, (h + 1) * MEM_HD)
        q = _rms(q_ref[:, cols].astype(F32), gq_ref[...]) * (MEM_HD ** -0.5)
        s = lax.dot_general(q.astype(BF16), k_ref[:, cols], NT_DIMS, preferred_element_type=F32)
        p = jnp.exp(s - jnp.max(s, axis=-1, keepdims=True))
        l = jnp.sum(p, axis=-1, keepdims=True)
        o = jnp.dot(p.astype(BF16), v_ref[:, cols], preferred_element_type=F32) / l
        o_ref[:, cols] = _rms(o, go_ref[:, cols]).astype(o_ref.dtype)


def mem_attn(proj, k_m, v_m, gq, go, *, batch, tm, col0):
    T = proj.shape[0]
    S = T // batch
    nm = k_m.shape[0] // batch
    width = MEM_HEADS * MEM_HD
    per_b = S // tm
    return pl.pallas_call(
        _mem_attn_kernel,
        grid=(T // tm,),
        in_specs=[
            pl.BlockSpec((tm, width), lambda i: (i, col0 // width)),
            pl.BlockSpec((nm, width), lambda i: (i // per_b, 0)),
            pl.BlockSpec((nm, width), lambda i: (i // per_b, 0)),
            pl.BlockSpec((1, MEM_HD), lambda i: (0, 0)),
            pl.BlockSpec((1, width), lambda i: (0, 0)),
        ],
        out_specs=pl.BlockSpec((tm, width), lambda i: (i, 0)),
        out_shape=jax.ShapeDtypeStruct((T, width), BF16),
        compiler_params=_cparams(("parallel",)),
        name="mem_attn",
    )(proj, k_m, v_m, gq, go)


def _out_proj_kernel(x_ref, ya_ref, yb_ref, yc_ref, wa_ref, wb_ref, wc_ref, o_ref):
    acc = jnp.dot(ya_ref[...], wa_ref[...], preferred_element_type=F32)
    acc += jnp.dot(yb_ref[...], wb_ref[...], preferred_element_type=F32)
    acc += jnp.dot(yc_ref[...], wc_ref[...], preferred_element_type=F32)
    o_ref[...] = x_ref[...] + acc


def out_proj(x, y_gla, y_na, y_mem, w_out, *, layer, tm, tn):
    T, D = x.shape
    wa, wb, wc = y_gla.shape[1], y_na.shape[1], y_mem.shape[1]
    assert wa % wb == 0 and wb == wc
    return pl.pallas_call(
        _out_proj_kernel,
        grid=(T // tm, D // tn),
        in_specs=[
            pl.BlockSpec((tm, tn), lambda i, j: (i, j)),
            pl.BlockSpec((tm, wa), lambda i, j: (i, 0)),
            pl.BlockSpec((tm, wb), lambda i, j: (i, 0)),
            pl.BlockSpec((tm, wc), lambda i, j: (i, 0)),
            pl.BlockSpec((None, wa, tn), lambda i, j: (layer, 0, j)),
            pl.BlockSpec((None, wb, tn), lambda i, j: (layer, wa // wb, j)),
            pl.BlockSpec((None, wc, tn), lambda i, j: (layer, wa // wb + 1, j)),
        ],
        out_specs=pl.BlockSpec((tm, tn), lambda i, j: (i, j)),
        out_shape=jax.ShapeDtypeStruct((T, D), F32),
        compiler_params=_cparams(("parallel", "arbitrary")),
        name="out_proj",
    )(x, y_gla, y_na, y_mem, w_out, w_out, w_out)


def _ffn_kernel(x_ref, g_ref, w1_ref, w3_ref, w2_ref, o_ref, h_ref):
    @pl.when(pl.program_id(1) == 0)
    def _():
        x = x_ref[...]
        h_ref[...] = _rms(x, g_ref[...]).astype(BF16)
        o_ref[...] = x

    h = h_ref[...]
    gate = jnp.dot(h, w1_ref[...], preferred_element_type=F32)
    up = jnp.dot(h, w3_ref[...], preferred_element_type=F32)
    act = (gate * jax.nn.sigmoid(gate) * up).astype(BF16)
    o_ref[...] += jnp.dot(act, w2_ref[...], preferred_element_type=F32)


def ffn(x, gain, w13, w2, *, layer, tm, tf):
    T, D = x.shape
    F = w2.shape[1]
    nf = F // tf
    return pl.pallas_call(
        _ffn_kernel,
        grid=(T // tm, nf),
        in_specs=[
            pl.BlockSpec((tm, D), lambda i, f: (i, 0)),
            pl.BlockSpec((1, D), lambda i, f: (0, 0)),
            pl.BlockSpec((None, D, tf), lambda i, f: (layer, 0, f)),
            pl.BlockSpec((None, D, tf), lambda i, f: (layer, 0, nf + f)),
            pl.BlockSpec((None, tf, D), lambda i, f: (layer, f, 0)),
        ],
        out_specs=pl.BlockSpec((tm, D), lambda i, f: (i, 0)),
        out_shape=jax.ShapeDtypeStruct((T, D), F32),
        scratch_shapes=[pltpu.VMEM((tm, D), BF16)],
        compiler_params=_cparams(("parallel", "arbitrary")),
        name="ffn",
    )(x, gain, w13, w13, w2)


IN_PROJ_TN = 1024


def kernel(x, mem, attn_norm, w_in, gla_wg2_f, gla_bg_f, gla_wg2_b, gla_bg_b, gla_out_norm,
           na_q_norm, na_k_norm, na_rpb, na_out_norm, mem_norm, mem_wkv, mem_q_norm, mem_k_norm,
           mem_out_norm, w_out, ffn_norm, ffn_w13, ffn_w2):
    B, S, D = x.shape
    depth = w_in.shape[0]
    T = B * S
    gla_dk = GLA_HEADS * GLA_HK
    gla_dv = GLA_HEADS * GLA_HV
    na_w = NA_HEADS * NA_HD
    mem_w = MEM_HEADS * MEM_HD
    gate0 = 2 * gla_dk + 2 * gla_dv
    gate1 = gate0 + 2 * GLA_RANK
    na_col0 = gate0
    mem_col0 = gate0 + 3 * na_w
    assert w_in.shape[2] - 2 * GLA_RANK == mem_col0 + mem_w and gla_dv + na_w + mem_w == w_out.shape[1]
    assert na_col0 % IN_PROJ_TN == 0 and 2 * na_w == IN_PROJ_TN

    w_main = jnp.concatenate([w_in[:, :, :gate0], w_in[:, :, gate1:]], axis=2).astype(BF16)
    w_gate = jnp.pad(w_in[:, :, gate0:gate1], ((0, 0), (0, 0), (0, 128 - 2 * GLA_RANK))).astype(BF16)
    w_out_b = w_out.astype(BF16)
    w13_b = ffn_w13.astype(BF16)
    w2_b = ffn_w2.astype(BF16)
    wkv_b = mem_wkv.astype(BF16)
    gqk = jnp.concatenate([jnp.tile(na_q_norm * (NA_HD ** -0.5), (1, NA_HEADS)),
                           jnp.tile(na_k_norm, (1, NA_HEADS))], axis=1)
    wb = natten_bias_tables(na_rpb)

    xf = x.reshape(T, D)
    mem2d = mem.reshape(B * mem.shape[1], D)
    for l in range(depth):
        proj, gates = in_proj(xf, attn_norm[l][None], w_main, w_gate, gqk[l][None], layer=l,
                              tm=1024, tn=IN_PROJ_TN, qk_tile=na_col0 // IN_PROJ_TN)
        o_f = gla_direction(proj, gates, gla_wg2_f[l], gla_bg_f[l][None], batch=B, tb=512, reverse=False)
        y_gla = gla_direction(proj, gates, gla_wg2_b[l], gla_bg_b[l][None], batch=B, tb=512, reverse=True,
                              o_fwd=o_f, out_gain=gla_out_norm[l][None])
        y_na = natten(proj, wb, na_out_norm[l][None], layer=l, batch=B, col0=na_col0)
        k_m, v_m = mem_kv(mem2d, mem_norm[l][None], wkv_b, mem_k_norm[l][None], layer=l)
        y_mem = mem_attn(proj, k_m, v_m, mem_q_norm[l][None], mem_out_norm[l][None],
                         batch=B, tm=512, col0=mem_col0)
        xf = out_proj(xf, y_gla, y_na, y_mem, w_out_b, layer=l, tm=1024, tn=1024)
        xf = ffn(xf, ffn_norm[l][None], w13_b, w2_b, layer=l, tm=512, tf=512)
    return xf.reshape(B, S, D)
```

```python
import functools

import jax
import jax.numpy as jnp
from jax import lax
from jax.experimental import pallas as pl
from jax.experimental.pallas import tpu as pltpu

F32 = jnp.float32
BF16 = jnp.bfloat16

RMS_EPS = 1e-6
MASK_VALUE = -1e30

GRID_W = 64
GLA_HEADS = 4
GLA_HK = 128
GLA_HV = 256
GLA_RANK = 16
GLA_TAU = 16.0
GLA_CHUNK = 64
NA_HD = 64
NA_HEADS = 8
NA_ROWS = 8
NA_COLS = 16
MEM_HEADS = 4
MEM_HD = 128

V7X_VMEM_BYTES = 64 * 1024 * 1024
VMEM_LIMIT_BYTES = V7X_VMEM_BYTES - 8 * 1024 * 1024

NT_DIMS = (((1,), (1,)), ((), ()))


def _cparams(sem):
    return pltpu.CompilerParams(dimension_semantics=sem, vmem_limit_bytes=VMEM_LIMIT_BYTES)


def _rms(x, gain):
    ms = jnp.mean(x * x, axis=-1, keepdims=True)
    return (x * lax.rsqrt(ms + RMS_EPS)) * gain


def _head_pair_ones():
    r = lax.broadcasted_iota(jnp.int32, (2 * NA_HD, 2 * NA_HD), 0) // NA_HD
    c = lax.broadcasted_iota(jnp.int32, (2 * NA_HD, 2 * NA_HD), 1) // NA_HD
    return jnp.where(r == c, 1.0, 0.0).astype(BF16)


def _head_pair_rms(x, gain, ones_bd):
    ms = jnp.dot((x * x).astype(BF16), ones_bd, preferred_element_type=F32) * (1.0 / NA_HD)
    return (x * lax.rsqrt(ms + RMS_EPS)) * gain


def _in_proj_kernel(x_ref, g_ref, w_ref, wg_ref, gqk_ref, proj_ref, gate_ref, *, sub, tn, qk_tile):
    tm = x_ref.shape[0]
    ones_bd = _head_pair_ones()
    lw = 2 * NA_HD
    for s in range(tm // sub):
        rows = slice(s * sub, (s + 1) * sub)
        xn = _rms(x_ref[rows, :], g_ref[...]).astype(BF16)
        gate_ref[rows, :] = jnp.dot(xn, wg_ref[...], preferred_element_type=F32)
        for n in range(w_ref.shape[1] // tn):
            acc = jnp.dot(xn, w_ref[:, n * tn:(n + 1) * tn], preferred_element_type=F32)
            if n == qk_tile:
                for c in range(tn // lw):
                    cols = slice(c * lw, (c + 1) * lw)
                    proj_ref[rows, n * tn + c * lw:n * tn + (c + 1) * lw] = _head_pair_rms(
                        acc[:, cols], gqk_ref[:, cols], ones_bd).astype(BF16)
            else:
                proj_ref[rows, n * tn:(n + 1) * tn] = acc.astype(BF16)


def in_proj(x, gain, w_main, w_gate, gqk, *, layer, tm, sub, tn, qk_tile):
    T, D = x.shape
    N = w_main.shape[2]
    G = w_gate.shape[2]
    return pl.pallas_call(
        functools.partial(_in_proj_kernel, sub=sub, tn=tn, qk_tile=qk_tile),
        grid=(T // tm,),
        in_specs=[
            pl.BlockSpec((tm, D), lambda i: (i, 0)),
            pl.BlockSpec((1, D), lambda i: (0, 0)),
            pl.BlockSpec((None, D, N), lambda i: (layer, 0, 0), pipeline_mode=pl.Buffered(1)),
            pl.BlockSpec((None, D, G), lambda i: (layer, 0, 0), pipeline_mode=pl.Buffered(1)),
            pl.BlockSpec((1, tn), lambda i: (0, 0)),
        ],
        out_specs=[
            pl.BlockSpec((tm, N), lambda i: (i, 0)),
            pl.BlockSpec((tm, G), lambda i: (i, 0)),
        ],
        out_shape=[jax.ShapeDtypeStruct((T, N), BF16), jax.ShapeDtypeStruct((T, G), F32)],
        compiler_params=_cparams(("parallel",)),
        name="in_proj",
    )(x, gain, w_main, w_gate, gqk)


def _log_sigmoid(z):
    return jnp.minimum(z, 0.0) - jnp.log(1.0 + jnp.exp(-jnp.abs(z)))


def _gla_kernel(q_ref, k_ref, v_ref, gate_ref, wg_ref, bg_ref, *rest, reverse, final, gate_off):
    if final:
        r_ref, of_ref, gn_ref, o_ref, st_ref = rest
    else:
        o_ref, st_ref = rest
    C = GLA_CHUNK
    H, dk, dv = GLA_HEADS, GLA_HK, GLA_HV
    tb = q_ref.shape[0]
    nchunk = tb // C

    @pl.when(pl.program_id(1) == 0)
    def _():
        st_ref[...] = jnp.zeros_like(st_ref)

    g = gate_ref[:, gate_off:gate_off + GLA_RANK]
    z = jnp.dot(g.astype(BF16), wg_ref[...].astype(BF16), preferred_element_type=F32) + bg_ref[...]
    la = _log_sigmoid(z) * (1.0 / GLA_TAU)

    la_t = jnp.concatenate([la[c * C:(c + 1) * C, :] for c in range(nchunk)], axis=1)
    row = lax.broadcasted_iota(jnp.int32, (C, C), 0)
    col = lax.broadcasted_iota(jnp.int32, (C, C), 1)
    if reverse:
        tri = (col >= row)
        keep = col > row
    else:
        tri = (col <= row)
        keep = col <= row
    tri = jnp.where(tri, 1.0, 0.0).astype(BF16)
    la_hi = la_t.astype(BF16)
    la_lo = (la_t - la_hi.astype(F32)).astype(BF16)
    cum = (jnp.dot(tri, la_hi, preferred_element_type=F32)
           + jnp.dot(tri, la_lo, preferred_element_type=F32))

    order = range(nchunk - 1, -1, -1) if reverse else range(nchunk)
    for c in order:
        rows = slice(c * C, (c + 1) * C)
        for h in range(H):
            kcols = slice(h * dk, (h + 1) * dk)
            vcols = slice(h * dv, (h + 1) * dv)
            cum_c = cum[:, (c * H + h) * dk:(c * H + h + 1) * dk]
            last = cum_c[0:1, :] if reverse else cum_c[C - 1:C, :]
            qc = q_ref[rows, kcols].astype(F32) * (GLA_HK ** -0.5)
            kc = k_ref[rows, kcols].astype(F32)
            vc = v_ref[rows, vcols]
            q_e = (qc * jnp.exp(cum_c)).astype(BF16)
            k_e = (kc * jnp.exp(-cum_c)).astype(BF16)
            k_end = (kc * jnp.exp(last - cum_c)).astype(BF16)
            sc = lax.dot_general(q_e, k_e, NT_DIMS, preferred_element_type=F32)
            sc = jnp.where(keep, sc, 0.0).astype(BF16)
            st = st_ref[h]
            o = (jnp.dot(sc, vc, preferred_element_type=F32)
                 + lax.dot_general(q_e, st.astype(BF16), NT_DIMS, preferred_element_type=F32))
            v_t = vc.T
            st_ref[h] = st * jnp.exp(last) + jnp.dot(v_t, k_end, preferred_element_type=F32)
            if final:
                o = o + of_ref[rows, vcols]
                y = _rms(o, gn_ref[:, vcols])
                r = r_ref[rows, vcols].astype(F32)
                o_ref[rows, vcols] = (y * (r * jax.nn.sigmoid(r))).astype(o_ref.dtype)
            else:
                o_ref[rows, vcols] = o


def gla_direction(proj, gates, wg2, bg, *, batch, tb, reverse, o_fwd=None, out_gain=None):
    T = proj.shape[0]
    S = T // batch
    nb = S // tb
    final = o_fwd is not None
    H, dk, dv = GLA_HEADS, GLA_HK, GLA_HV
    wk, wv = H * dk, H * dv

    def tok(b, i):
        return b * nb + ((nb - 1 - i) if reverse else i)

    in_specs = [
        pl.BlockSpec((tb, wk), lambda b, i: (tok(b, i), 0)),
        pl.BlockSpec((tb, wk), lambda b, i: (tok(b, i), 1)),
        pl.BlockSpec((tb, wv), lambda b, i: (tok(b, i), (2 * wk) // wv)),
        pl.BlockSpec((tb, gates.shape[1]), lambda b, i: (tok(b, i), 0)),
        pl.BlockSpec((GLA_RANK, wk), lambda b, i: (0, 0)),
        pl.BlockSpec((1, wk), lambda b, i: (0, 0)),
    ]
    args = [proj, proj, proj, gates, wg2, bg]
    if final:
        in_specs += [
            pl.BlockSpec((tb, wv), lambda b, i: (tok(b, i), (2 * wk) // wv + 1)),
            pl.BlockSpec((tb, wv), lambda b, i: (tok(b, i), 0)),
            pl.BlockSpec((1, wv), lambda b, i: (0, 0)),
        ]
        args += [proj, o_fwd, out_gain]
    kern = functools.partial(_gla_kernel, reverse=reverse, final=final,
                             gate_off=GLA_RANK if reverse else 0)
    return pl.pallas_call(
        kern,
        grid=(batch, nb),
        in_specs=in_specs,
        out_specs=pl.BlockSpec((tb, wv), lambda b, i: (tok(b, i), 0)),
        out_shape=jax.ShapeDtypeStruct((T, wv), BF16 if final else F32),
        scratch_shapes=[pltpu.VMEM((H, dv, dk), F32)],
        compiler_params=_cparams(("parallel", "arbitrary")),
        name="gla_bwd" if reverse else "gla_fwd",
    )(*args)


NA_QROWS = 4
NA_DR = 2 * NA_ROWS - 1
NA_DC = 2 * NA_COLS - 1


def _natten_bias_kernel(rpb_ref, o_ref):
    W = GRID_W
    base = (pl.program_id(0) * NA_HEADS + pl.program_id(1)) * (NA_DR * NA_DC)
    qc = lax.broadcasted_iota(jnp.int32, (W, 2 * W), 0)
    ln = lax.broadcasted_iota(jnp.int32, (W, 2 * W), 1)
    kc = ln & (W - 1)
    dc_idx = jnp.clip(kc - qc, -(NA_COLS - 1), NA_COLS - 1) + (NA_COLS - 1)
    cs = jnp.clip(qc - NA_COLS // 2, 0, W - NA_COLS)
    in_win = (kc >= cs) & (kc < cs + NA_COLS)
    left = ln < W
    toeplitz = []
    for dr in range(NA_DR):
        t = jnp.full((W, 2 * W), MASK_VALUE, F32)
        for dc in range(NA_DC):
            t = jnp.where(dc_idx == dc, rpb_ref[base + dr * NA_DC + dc], t)
        toeplitz.append(jnp.where(in_win, t, MASK_VALUE))
    for dr0 in range(NA_ROWS):
        for jp in range(NA_ROWS // 2):
            o_ref[dr0, :, jp * 2 * W:(jp + 1) * 2 * W] = jnp.where(
                left, toeplitz[dr0 + 2 * jp], toeplitz[dr0 + 2 * jp + 1])


def natten_bias_tables(rpb):
    L, H = rpb.shape[:2]
    W = GRID_W
    return pl.pallas_call(
        _natten_bias_kernel,
        grid=(L, H),
        in_specs=[pl.BlockSpec(memory_space=pltpu.SMEM)],
        out_specs=pl.BlockSpec((None, None, NA_ROWS, W, NA_ROWS * W), lambda l, h: (l, h, 0, 0, 0)),
        out_shape=jax.ShapeDtypeStruct((L, H, NA_ROWS, W, NA_ROWS * W), F32),
        compiler_params=_cparams(("parallel", "parallel")),
        name="natten_bias",
    )(rpb.reshape(-1))


def _natten_kernel(q_ref, kp_ref, kc_ref, kn_ref, vp_ref, vc_ref, vn_ref, wb_ref, go_ref, o_ref,
                   k_all, v_all, o_scr, *, rows):
    W = GRID_W
    nq = q_ref.shape[0]
    lw = 2 * NA_HD
    nwin = NA_ROWS * W
    k_all[0:nq] = kp_ref[...]
    k_all[nq:2 * nq] = kc_ref[...]
    k_all[2 * nq:3 * nq] = kn_ref[...]
    v_all[0:nq] = vp_ref[...]
    v_all[nq:2 * nq] = vc_ref[...]
    v_all[2 * nq:3 * nq] = vn_ref[...]

    head0 = lax.broadcasted_iota(jnp.int32, (1, lw), 1) < NA_HD
    ones_bd = _head_pair_ones()
    blk = pl.program_id(1)
    nblk = rows // NA_QROWS

    def step(r0):
        for a in range(NA_QROWS):
            rs = min(max(r0 + a - NA_ROWS // 2, 0), rows - NA_ROWS)
            off = rs - (r0 - NA_QROWS)
            dr0 = rs - (r0 + a) + (NA_ROWS - 1)
            krows = slice(off * W, off * W + nwin)
            qrows = slice(a * W, (a + 1) * W)
            pairs = range(NA_HEADS // 2)
            cols = [slice(p * lw, (p + 1) * lw) for p in pairs]
            s = []
            for p in pairs:
                qa = q_ref[qrows, cols[p]]
                zero = jnp.zeros_like(qa)
                lhs = jnp.concatenate([jnp.where(head0, qa, zero), jnp.where(head0, zero, qa)], axis=0)
                sp = lax.dot_general(lhs, k_all[krows, cols[p]], NT_DIMS,
                                     preferred_element_type=F32)
                s.append(sp + jnp.concatenate([wb_ref[2 * p, dr0], wb_ref[2 * p + 1, dr0]], axis=0))
            m = [jnp.max(sp, axis=-1, keepdims=True) for sp in s]
            e = [jnp.exp(sp - mp) for sp, mp in zip(s, m)]
            l = [jnp.sum(ep, axis=-1, keepdims=True) for ep in e]
            r = [jnp.dot(e[p].astype(BF16), v_all[krows, cols[p]], preferred_element_type=F32) for p in pairs]
            for p in pairs:
                rp = r[p] / l[p]
                o_scr[qrows, cols[p]] = jnp.where(head0, rp[0:W], rp[W:2 * W])

    pl.when(blk == 0)(lambda: step(0))
    pl.when(jnp.logical_and(blk > 0, blk < nblk - 1))(lambda: step(NA_QROWS))
    pl.when(blk == nblk - 1)(lambda: step(rows - NA_QROWS))

    for p in range(NA_HEADS // 2):
        cols = slice(p * lw, (p + 1) * lw)
        o_ref[:, cols] = _head_pair_rms(o_scr[:, cols], go_ref[:, cols], ones_bd).astype(o_ref.dtype)


def natten(proj, wb, go, *, layer, batch, col0):
    T = proj.shape[0]
    S = T // batch
    rows = S // GRID_W
    nq = NA_QROWS * GRID_W
    nblk = S // nq
    width = NA_HEADS * NA_HD
    cq = col0 // width

    def prev(i):
        return jnp.maximum(i - 1, 0)

    def nxt(i):
        return jnp.minimum(i + 1, nblk - 1)

    in_specs = [
        pl.BlockSpec((nq, width), lambda b, i: (b * nblk + i, cq)),
        pl.BlockSpec((nq, width), lambda b, i: (b * nblk + prev(i), cq + 1)),
        pl.BlockSpec((nq, width), lambda b, i: (b * nblk + i, cq + 1)),
        pl.BlockSpec((nq, width), lambda b, i: (b * nblk + nxt(i), cq + 1)),
        pl.BlockSpec((nq, width), lambda b, i: (b * nblk + prev(i), cq + 2)),
        pl.BlockSpec((nq, width), lambda b, i: (b * nblk + i, cq + 2)),
        pl.BlockSpec((nq, width), lambda b, i: (b * nblk + nxt(i), cq + 2)),
        pl.BlockSpec((None,) + wb.shape[1:], lambda b, i: (layer, 0, 0, 0, 0)),
        pl.BlockSpec((1, width), lambda b, i: (0, 0)),
    ]
    return pl.pallas_call(
        functools.partial(_natten_kernel, rows=rows),
        grid=(batch, nblk),
        in_specs=in_specs,
        out_specs=pl.BlockSpec((nq, width), lambda b, i: (b * nblk + i, 0)),
        out_shape=jax.ShapeDtypeStruct((T, width), BF16),
        scratch_shapes=[pltpu.VMEM((3 * nq, width), BF16), pltpu.VMEM((3 * nq, width), BF16),
                        pltpu.VMEM((nq, width), F32)],
        compiler_params=_cparams(("parallel", "arbitrary")),
        name="natten",
    )(proj, proj, proj, proj, proj, proj, proj, wb, go)


def _mem_kv_kernel(mem_ref, g_ref, w_ref, gk_ref, k_ref, v_ref):
    mn = _rms(mem_ref[...], g_ref[...]).astype(BF16)
    kv = jnp.dot(mn, w_ref[...], preferred_element_type=F32)
    width = k_ref.shape[1]
    for h in range(MEM_HEADS):
        cols = slice(h * MEM_HD, (h + 1) * MEM_HD)
        k_ref[:, cols] = _rms(kv[:, cols], gk_ref[...]).astype(BF16)
    v_ref[...] = kv[:, width:].astype(BF16)


def mem_kv(mem2d, gain, wkv, gk, *, layer):
    n, D = mem2d.shape
    width = wkv.shape[2] // 2
    return pl.pallas_call(
        _mem_kv_kernel,
        grid=(1,),
        in_specs=[
            pl.BlockSpec((n, D), lambda i: (0, 0)),
            pl.BlockSpec((1, D), lambda i: (0, 0)),
            pl.BlockSpec((None, D, 2 * width), lambda i: (layer, 0, 0)),
            pl.BlockSpec((1, MEM_HD), lambda i: (0, 0)),
        ],
        out_specs=[pl.BlockSpec((n, width), lambda i: (0, 0)), pl.BlockSpec((n, width), lambda i: (0, 0))],
        out_shape=[jax.ShapeDtypeStruct((n, width), BF16), jax.ShapeDtypeStruct((n, width), BF16)],
        compiler_params=_cparams(("arbitrary",)),
        name="mem_kv",
    )(mem2d, gain, wkv, gk)


def _mem_attn_kernel(q_ref, k_ref, v_ref, gq_ref, go_ref, o_ref):
    for h in range(MEM_HEADS):
        cols = slice(h * MEM_HD, (h + 1) * MEM_HD)
        q = _rms(q_ref[:, cols].astype(F32), gq_ref[...]) * (MEM_HD ** -0.5)
        s = lax.dot_general(q.astype(BF16), k_ref[:, cols], NT_DIMS, preferred_element_type=F32)
        p = jnp.exp(s - jnp.max(s, axis=-1, keepdims=True))
        l = jnp.sum(p, axis=-1, keepdims=True)
        o = jnp.dot(p.astype(BF16), v_ref[:, cols], preferred_element_type=F32) / l
        o_ref[:, cols] = _rms(o, go_ref[:, cols]).astype(o_ref.dtype)


def mem_attn(proj, k_m, v_m, gq, go, *, batch, tm, col0):
    T = proj.shape[0]
    S = T // batch
    nm = k_m.shape[0] // batch
    width = MEM_HEADS * MEM_HD
    per_b = S // tm
    return pl.pallas_call(
        _mem_attn_kernel,
        grid=(T // tm,),
        in_specs=[
            pl.BlockSpec((tm, width), lambda i: (i, col0 // width)),
            pl.BlockSpec((nm, width), lambda i: (i // per_b, 0)),
            pl.BlockSpec((nm, width), lambda i: (i // per_b, 0)),
            pl.BlockSpec((1, MEM_HD), lambda i: (0, 0)),
            pl.BlockSpec((1, width), lambda i: (0, 0)),
        ],
        out_specs=pl.BlockSpec((tm, width), lambda i: (i, 0)),
        out_shape=jax.ShapeDtypeStruct((T, width), BF16),
        compiler_params=_cparams(("parallel",)),
        name="mem_attn",
    )(proj, k_m, v_m, gq, go)


def _out_proj_kernel(x_ref, ya_ref, yb_ref, yc_ref, w_ref, g_ref, o_ref, h_ref, *, sub):
    tm = x_ref.shape[0]
    wa, wb = ya_ref.shape[1], yb_ref.shape[1]
    for s in range(tm // sub):
        rows = slice(s * sub, (s + 1) * sub)
        acc = x_ref[rows, :] + jnp.dot(ya_ref[rows, :], w_ref[0:wa, :], preferred_element_type=F32)
        acc += jnp.dot(yb_ref[rows, :], w_ref[wa:wa + wb, :], preferred_element_type=F32)
        acc += jnp.dot(yc_ref[rows, :], w_ref[wa + wb:, :], preferred_element_type=F32)
        o_ref[rows, :] = acc
        h_ref[rows, :] = _rms(acc, g_ref[...]).astype(BF16)


def out_proj(x, y_gla, y_na, y_mem, w_out, gain, *, layer, tm, sub):
    T, D = x.shape
    wa, wb, wc = y_gla.shape[1], y_na.shape[1], y_mem.shape[1]
    return pl.pallas_call(
        functools.partial(_out_proj_kernel, sub=sub),
        grid=(T // tm,),
        in_specs=[
            pl.BlockSpec((tm, D), lambda i: (i, 0)),
            pl.BlockSpec((tm, wa), lambda i: (i, 0)),
            pl.BlockSpec((tm, wb), lambda i: (i, 0)),
            pl.BlockSpec((tm, wc), lambda i: (i, 0)),
            pl.BlockSpec((None, wa + wb + wc, D), lambda i: (layer, 0, 0), pipeline_mode=pl.Buffered(1)),
            pl.BlockSpec((1, D), lambda i: (0, 0)),
        ],
        out_specs=[pl.BlockSpec((tm, D), lambda i: (i, 0)), pl.BlockSpec((tm, D), lambda i: (i, 0))],
        out_shape=[jax.ShapeDtypeStruct((T, D), F32), jax.ShapeDtypeStruct((T, D), BF16)],
        compiler_params=_cparams(("parallel",)),
        name="out_proj",
    )(x, y_gla, y_na, y_mem, w_out, gain)


def _ffn_kernel(x_ref, h_ref, w1_ref, w3_ref, w2_ref, o_ref):
    @pl.when(pl.program_id(1) == 0)
    def _():
        o_ref[...] = x_ref[...]

    h = h_ref[...]
    gate = jnp.dot(h, w1_ref[...], preferred_element_type=F32)
    up = jnp.dot(h, w3_ref[...], preferred_element_type=F32)
    act = (gate * jax.nn.sigmoid(gate) * up).astype(BF16)
    o_ref[...] += jnp.dot(act, w2_ref[...], preferred_element_type=F32)


def ffn(x, h, w13, w2, *, layer, tm, tf):
    T, D = x.shape
    F = w2.shape[1]
    nf = F // tf
    return pl.pallas_call(
        _ffn_kernel,
        grid=(T // tm, nf),
        in_specs=[
            pl.BlockSpec((tm, D), lambda i, f: (i, 0)),
            pl.BlockSpec((tm, D), lambda i, f: (i, 0)),
            pl.BlockSpec((None, D, tf), lambda i, f: (layer, 0, f)),
            pl.BlockSpec((None, D, tf), lambda i, f: (layer, 0, nf + f)),
            pl.BlockSpec((None, tf, D), lambda i, f: (layer, f, 0)),
        ],
        out_specs=pl.BlockSpec((tm, D), lambda i, f: (i, 0)),
        out_shape=jax.ShapeDtypeStruct((T, D), F32),
        compiler_params=_cparams(("parallel", "arbitrary")),
        name="ffn",
    )(x, h, w13, w13, w2)


IN_PROJ_TN = 1024


def kernel(x, mem, attn_norm, w_in, gla_wg2_f, gla_bg_f, gla_wg2_b, gla_bg_b, gla_out_norm,
           na_q_norm, na_k_norm, na_rpb, na_out_norm, mem_norm, mem_wkv, mem_q_norm, mem_k_norm,
           mem_out_norm, w_out, ffn_norm, ffn_w13, ffn_w2):
    B, S, D = x.shape
    depth = w_in.shape[0]
    T = B * S
    gla_dk = GLA_HEADS * GLA_HK
    gla_dv = GLA_HEADS * GLA_HV
    na_w = NA_HEADS * NA_HD
    mem_w = MEM_HEADS * MEM_HD
    gate0 = 2 * gla_dk + 2 * gla_dv
    gate1 = gate0 + 2 * GLA_RANK
    na_col0 = gate0
    mem_col0 = gate0 + 3 * na_w
    assert w_in.shape[2] - 2 * GLA_RANK == mem_col0 + mem_w and gla_dv + na_w + mem_w == w_out.shape[1]
    assert na_col0 % IN_PROJ_TN == 0 and 2 * na_w == IN_PROJ_TN

    w_in_b = w_in.astype(BF16)
    w_main = jnp.concatenate([w_in_b[:, :, :gate0], w_in_b[:, :, gate1:]], axis=2)
    w_gate = jnp.pad(w_in_b[:, :, gate0:gate1], ((0, 0), (0, 0), (0, 128 - 2 * GLA_RANK)))
    w_out_b = w_out.astype(BF16)
    w13_b = ffn_w13.astype(BF16)
    w2_b = ffn_w2.astype(BF16)
    wkv_b = mem_wkv.astype(BF16)
    gqk = jnp.concatenate([jnp.tile(na_q_norm * (NA_HD ** -0.5), (1, NA_HEADS)),
                           jnp.tile(na_k_norm, (1, NA_HEADS))], axis=1)
    wb = natten_bias_tables(na_rpb)

    xf = x.reshape(T, D)
    mem2d = mem.reshape(B * mem.shape[1], D)
    for l in range(depth):
        proj, gates = in_proj(xf, attn_norm[l][None], w_main, w_gate, gqk[l][None], layer=l,
                              tm=512, sub=256, tn=IN_PROJ_TN, qk_tile=na_col0 // IN_PROJ_TN)
        o_f = gla_direction(proj, gates, gla_wg2_f[l], gla_bg_f[l][None], batch=B, tb=512, reverse=False)
        y_gla = gla_direction(proj, gates, gla_wg2_b[l], gla_bg_b[l][None], batch=B, tb=512, reverse=True,
                              o_fwd=o_f, out_gain=gla_out_norm[l][None])
        y_na = natten(proj, wb, na_out_norm[l][None], layer=l, batch=B, col0=na_col0)
        k_m, v_m = mem_kv(mem2d, mem_norm[l][None], wkv_b, mem_k_norm[l][None], layer=l)
        y_mem = mem_attn(proj, k_m, v_m, mem_q_norm[l][None], mem_out_norm[l][None],
                         batch=B, tm=512, col0=mem_col0)
        xf, hf = out_proj(xf, y_gla, y_na, y_mem, w_out_b, ffn_norm[l][None], layer=l, tm=512, sub=256)
        xf = ffn(xf, hf, w13_b, w2_b, layer=l, tm=512, tf=512)
    return xf.reshape(B, S, D)
```

```python
import functools

import jax
import jax.numpy as jnp
from jax import lax
from jax.experimental import pallas as pl
from jax.experimental.pallas import tpu as pltpu

F32 = jnp.float32
BF16 = jnp.bfloat16

RMS_EPS = 1e-6
MASK_VALUE = -1e30

GRID_W = 64
GLA_HEADS = 4
GLA_HK = 128
GLA_HV = 256
GLA_RANK = 16
GLA_TAU = 16.0
GLA_CHUNK = 64
NA_HD = 64
NA_HEADS = 8
NA_ROWS = 8
NA_COLS = 16
MEM_HEADS = 4
MEM_HD = 128

V7X_VMEM_BYTES = 64 * 1024 * 1024
VMEM_LIMIT_BYTES = V7X_VMEM_BYTES - 8 * 1024 * 1024

NT_DIMS = (((1,), (1,)), ((), ()))


def _cparams(sem):
    return pltpu.CompilerParams(dimension_semantics=sem, vmem_limit_bytes=VMEM_LIMIT_BYTES)


def _rms(x, gain):
    ms = jnp.mean(x * x, axis=-1, keepdims=True)
    return (x * lax.rsqrt(ms + RMS_EPS)) * gain


def _head_pair_ones():
    r = lax.broadcasted_iota(jnp.int32, (2 * NA_HD, 2 * NA_HD), 0) // NA_HD
    c = lax.broadcasted_iota(jnp.int32, (2 * NA_HD, 2 * NA_HD), 1) // NA_HD
    return jnp.where(r == c, 1.0, 0.0).astype(BF16)


def _head_pair_rms(x, gain, ones_bd):
    ms = jnp.dot((x * x).astype(BF16), ones_bd, preferred_element_type=F32) * (1.0 / NA_HD)
    return (x * lax.rsqrt(ms + RMS_EPS)) * gain


def _w_main_kernel(wa_ref, wb_ref, o_ref, *, first_shifted, shift):
    j = pl.program_id(2)
    tn = o_ref.shape[1]

    @pl.when(j < first_shifted)
    def _():
        o_ref[...] = wa_ref[...].astype(BF16)

    @pl.when(j >= first_shifted)
    def _():
        full = jnp.concatenate([wa_ref[...], wb_ref[...]], axis=1)
        o_ref[...] = full[:, shift:shift + tn].astype(BF16)


def w_main_from_w_in(w_in, *, gate0, ngate, tr, tn):
    L, D, n_in = w_in.shape
    n_out = n_in - ngate
    assert gate0 % tn == 0 and n_out % tn == 0 and ngate < 128 and tn % 128 == 0
    first_shifted = gate0 // tn
    nb128 = tn // 128

    def next_cols(l, r, j):
        return (l, r, jnp.where(j >= first_shifted, (j + 1) * nb128, 0))

    return pl.pallas_call(
        functools.partial(_w_main_kernel, first_shifted=first_shifted, shift=ngate),
        grid=(L, D // tr, n_out // tn),
        in_specs=[
            pl.BlockSpec((None, tr, tn), lambda l, r, j: (l, r, j)),
            pl.BlockSpec((None, tr, 128), next_cols),
        ],
        out_specs=pl.BlockSpec((None, tr, tn), lambda l, r, j: (l, r, j)),
        out_shape=jax.ShapeDtypeStruct((L, D, n_out), BF16),
        compiler_params=_cparams(("parallel", "parallel", "parallel")),
        name="w_main",
    )(w_in, w_in)


def _in_proj_kernel(x_ref, g_ref, w_ref, wg_ref, gqk_ref, proj_ref, gate_ref, *, sub, tn, qk_tile):
    tm = x_ref.shape[0]
    ones_bd = _head_pair_ones()
    lw = 2 * NA_HD
    for s in range(tm // sub):
        rows = slice(s * sub, (s + 1) * sub)
        xn = _rms(x_ref[rows, :], g_ref[...]).astype(BF16)
        gate_ref[rows, :] = jnp.dot(xn, wg_ref[...], preferred_element_type=F32)
        for n in range(w_ref.shape[1] // tn):
            acc = jnp.dot(xn, w_ref[:, n * tn:(n + 1) * tn], preferred_element_type=F32)
            if n == qk_tile:
                for c in range(tn // lw):
                    cols = slice(c * lw, (c + 1) * lw)
                    proj_ref[rows, n * tn + c * lw:n * tn + (c + 1) * lw] = _head_pair_rms(
                        acc[:, cols], gqk_ref[:, cols], ones_bd).astype(BF16)
            else:
                proj_ref[rows, n * tn:(n + 1) * tn] = acc.astype(BF16)


def in_proj(x, gain, w_main, w_gate, gqk, *, layer, tm, sub, tn, qk_tile):
    T, D = x.shape
    N = w_main.shape[2]
    G = w_gate.shape[2]
    return pl.pallas_call(
        functools.partial(_in_proj_kernel, sub=sub, tn=tn, qk_tile=qk_tile),
        grid=(T // tm,),
        in_specs=[
            pl.BlockSpec((tm, D), lambda i: (i, 0)),
            pl.BlockSpec((1, D), lambda i: (0, 0)),
            pl.BlockSpec((None, D, N), lambda i: (layer, 0, 0), pipeline_mode=pl.Buffered(1)),
            pl.BlockSpec((None, D, G), lambda i: (layer, 0, 0), pipeline_mode=pl.Buffered(1)),
            pl.BlockSpec((1, tn), lambda i: (0, 0)),
        ],
        out_specs=[
            pl.BlockSpec((tm, N), lambda i: (i, 0)),
            pl.BlockSpec((tm, G), lambda i: (i, 0)),
        ],
        out_shape=[jax.ShapeDtypeStruct((T, N), BF16), jax.ShapeDtypeStruct((T, G), F32)],
        compiler_params=_cparams(("parallel",)),
        name="in_proj",
    )(x, gain, w_main, w_gate, gqk)


def _log_sigmoid(z):
    return jnp.minimum(z, 0.0) - jnp.log(1.0 + jnp.exp(-jnp.abs(z)))


def _gla_kernel(q_ref, k_ref, v_ref, gate_ref, wg_ref, bg_ref, *rest, reverse, final, gate_off):
    if final:
        r_ref, of_ref, gn_ref, o_ref, st_ref = rest
    else:
        o_ref, st_ref = rest
    C = GLA_CHUNK
    H, dk, dv = GLA_HEADS, GLA_HK, GLA_HV
    tb = q_ref.shape[0]
    nchunk = tb // C

    @pl.when(pl.program_id(1) == 0)
    def _():
        st_ref[...] = jnp.zeros_like(st_ref)

    g = gate_ref[:, gate_off:gate_off + GLA_RANK]
    z = jnp.dot(g.astype(BF16), wg_ref[...].astype(BF16), preferred_element_type=F32) + bg_ref[...]
    la = _log_sigmoid(z) * (1.0 / GLA_TAU)

    la_t = jnp.concatenate([la[c * C:(c + 1) * C, :] for c in range(nchunk)], axis=1)
    row = lax.broadcasted_iota(jnp.int32, (C, C), 0)
    col = lax.broadcasted_iota(jnp.int32, (C, C), 1)
    if reverse:
        tri = (col >= row)
        keep = col > row
    else:
        tri = (col <= row)
        keep = col <= row
    tri = jnp.where(tri, 1.0, 0.0).astype(BF16)
    la_hi = la_t.astype(BF16)
    la_lo = (la_t - la_hi.astype(F32)).astype(BF16)
    cum = (jnp.dot(tri, la_hi, preferred_element_type=F32)
           + jnp.dot(tri, la_lo, preferred_element_type=F32))

    order = range(nchunk - 1, -1, -1) if reverse else range(nchunk)
    heads = range(H)
    kcols = [slice(h * dk, (h + 1) * dk) for h in heads]
    vcols = [slice(h * dv, (h + 1) * dv) for h in heads]
    for c in order:
        rows = slice(c * C, (c + 1) * C)
        cum_c = cum[:, c * H * dk:(c + 1) * H * dk]
        last = cum_c[0:1, :] if reverse else cum_c[C - 1:C, :]
        qc = q_ref[rows, :].astype(F32) * (GLA_HK ** -0.5)
        kc = k_ref[rows, :].astype(F32)
        q_e = (qc * jnp.exp(cum_c)).astype(BF16)
        k_e = (kc * jnp.exp(-cum_c)).astype(BF16)
        k_end = (kc * jnp.exp(last - cum_c)).astype(BF16)
        decay = jnp.exp(last)
        vc = [v_ref[rows, vcols[h]] for h in heads]
        sc = [lax.dot_general(q_e[:, kcols[h]], k_e[:, kcols[h]], NT_DIMS, preferred_element_type=F32)
              for h in heads]
        sc = [jnp.where(keep, s, 0.0).astype(BF16) for s in sc]
        st = [st_ref[h] for h in heads]
        o = [jnp.dot(sc[h], vc[h], preferred_element_type=F32)
             + lax.dot_general(q_e[:, kcols[h]], st[h].astype(BF16), NT_DIMS, preferred_element_type=F32)
             for h in heads]
        for h in heads:
            st_ref[h] = (st[h] * decay[:, kcols[h]]
                         + jnp.dot(vc[h].T, k_end[:, kcols[h]], preferred_element_type=F32))
        for h in heads:
            if final:
                oh = o[h] + of_ref[rows, vcols[h]]
                y = _rms(oh, gn_ref[:, vcols[h]])
                r = r_ref[rows, vcols[h]].astype(F32)
                o_ref[rows, vcols[h]] = (y * (r * jax.nn.sigmoid(r))).astype(o_ref.dtype)
            else:
                o_ref[rows, vcols[h]] = o[h]


def gla_direction(proj, gates, wg2, bg, *, batch, tb, reverse, o_fwd=None, out_gain=None):
    T = proj.shape[0]
    S = T // batch
    nb = S // tb
    final = o_fwd is not None
    H, dk, dv = GLA_HEADS, GLA_HK, GLA_HV
    wk, wv = H * dk, H * dv

    def tok(b, i):
        return b * nb + ((nb - 1 - i) if reverse else i)

    in_specs = [
        pl.BlockSpec((tb, wk), lambda b, i: (tok(b, i), 0)),
        pl.BlockSpec((tb, wk), lambda b, i: (tok(b, i), 1)),
        pl.BlockSpec((tb, wv), lambda b, i: (tok(b, i), (2 * wk) // wv)),
        pl.BlockSpec((tb, gates.shape[1]), lambda b, i: (tok(b, i), 0)),
        pl.BlockSpec((GLA_RANK, wk), lambda b, i: (0, 0)),
        pl.BlockSpec((1, wk), lambda b, i: (0, 0)),
    ]
    args = [proj, proj, proj, gates, wg2, bg]
    if final:
        in_specs += [
            pl.BlockSpec((tb, wv), lambda b, i: (tok(b, i), (2 * wk) // wv + 1)),
            pl.BlockSpec((tb, wv), lambda b, i: (tok(b, i), 0)),
            pl.BlockSpec((1, wv), lambda b, i: (0, 0)),
        ]
        args += [proj, o_fwd, out_gain]
    kern = functools.partial(_gla_kernel, reverse=reverse, final=final,
                             gate_off=GLA_RANK if reverse else 0)
    return pl.pallas_call(
        kern,
        grid=(batch, nb),
        in_specs=in_specs,
        out_specs=pl.BlockSpec((tb, wv), lambda b, i: (tok(b, i), 0)),
        out_shape=jax.ShapeDtypeStruct((T, wv), BF16 if final else F32),
        scratch_shapes=[pltpu.VMEM((H, dv, dk), F32)],
        compiler_params=_cparams(("parallel", "arbitrary")),
        name="gla_bwd" if reverse else "gla_fwd",
    )(*args)


NA_QROWS = 8
NA_DR = 2 * NA_ROWS - 1
NA_DC = 2 * NA_COLS - 1


def _natten_bias_kernel(rpb_ref, o_ref):
    W = GRID_W
    base = (pl.program_id(0) * NA_HEADS + pl.program_id(1)) * (NA_DR * NA_DC)
    qc = lax.broadcasted_iota(jnp.int32, (W, 2 * W), 0)
    ln = lax.broadcasted_iota(jnp.int32, (W, 2 * W), 1)
    kc = ln & (W - 1)
    dc_idx = jnp.clip(kc - qc, -(NA_COLS - 1), NA_COLS - 1) + (NA_COLS - 1)
    cs = jnp.clip(qc - NA_COLS // 2, 0, W - NA_COLS)
    in_win = (kc >= cs) & (kc < cs + NA_COLS)
    left = ln < W
    toeplitz = []
    for dr in range(NA_DR):
        t = jnp.full((W, 2 * W), MASK_VALUE, F32)
        for dc in range(NA_DC):
            t = jnp.where(dc_idx == dc, rpb_ref[base + dr * NA_DC + dc], t)
        toeplitz.append(jnp.where(in_win, t, MASK_VALUE))
    for dr0 in range(NA_ROWS):
        for jp in range(NA_ROWS // 2):
            o_ref[dr0, :, jp * 2 * W:(jp + 1) * 2 * W] = jnp.where(
                left, toeplitz[dr0 + 2 * jp], toeplitz[dr0 + 2 * jp + 1])


def natten_bias_tables(rpb):
    L, H = rpb.shape[:2]
    W = GRID_W
    return pl.pallas_call(
        _natten_bias_kernel,
        grid=(L, H),
        in_specs=[pl.BlockSpec(memory_space=pltpu.SMEM)],
        out_specs=pl.BlockSpec((None, None, NA_ROWS, W, NA_ROWS * W), lambda l, h: (l, h, 0, 0, 0)),
        out_shape=jax.ShapeDtypeStruct((L, H, NA_ROWS, W, NA_ROWS * W), F32),
        compiler_params=_cparams(("parallel", "parallel")),
        name="natten_bias",
    )(rpb.reshape(-1))


def _natten_kernel(q_ref, kp_ref, kc_ref, kn_ref, vp_ref, vc_ref, vn_ref, wb_ref, go_ref, o_ref,
                   o_scr, *, rows):
    W = GRID_W
    nq = q_ref.shape[0]
    lw = 2 * NA_HD
    nwin = NA_ROWS * W

    def window(refs, start, cols):
        parts = []
        for b, ref in enumerate(refs):
            lo, hi = max(start, b * nq), min(start + nwin, (b + 1) * nq)
            if lo < hi:
                parts.append(ref[lo - b * nq:hi - b * nq, cols])
        return parts[0] if len(parts) == 1 else jnp.concatenate(parts, axis=0)

    k_refs = (kp_ref, kc_ref, kn_ref)
    v_refs = (vp_ref, vc_ref, vn_ref)
    head0 = lax.broadcasted_iota(jnp.int32, (1, lw), 1) < NA_HD
    ones_bd = _head_pair_ones()
    blk = pl.program_id(1)
    nblk = rows // NA_QROWS

    def step(r0):
        for a in range(NA_QROWS):
            rs = min(max(r0 + a - NA_ROWS // 2, 0), rows - NA_ROWS)
            off = rs - (r0 - NA_QROWS)
            dr0 = rs - (r0 + a) + (NA_ROWS - 1)
            qrows = slice(a * W, (a + 1) * W)
            pairs = range(NA_HEADS // 2)
            cols = [slice(p * lw, (p + 1) * lw) for p in pairs]
            s = []
            for p in pairs:
                qa = q_ref[qrows, cols[p]]
                zero = jnp.zeros_like(qa)
                lhs = jnp.concatenate([jnp.where(head0, qa, zero), jnp.where(head0, zero, qa)], axis=0)
                sp = lax.dot_general(lhs, window(k_refs, off * W, cols[p]), NT_DIMS,
                                     preferred_element_type=F32)
                s.append(sp + jnp.concatenate([wb_ref[2 * p, dr0], wb_ref[2 * p + 1, dr0]], axis=0))
            m = [jnp.max(sp, axis=-1, keepdims=True) for sp in s]
            e = [jnp.exp(sp - mp) for sp, mp in zip(s, m)]
            l = [jnp.sum(ep, axis=-1, keepdims=True) for ep in e]
            r = [jnp.dot(e[p].astype(BF16), window(v_refs, off * W, cols[p]), preferred_element_type=F32)
                 for p in pairs]
            for p in pairs:
                rp = r[p] / l[p]
                o_scr[qrows, cols[p]] = jnp.where(head0, rp[0:W], rp[W:2 * W])

    pl.when(blk == 0)(lambda: step(0))
    pl.when(jnp.logical_and(blk > 0, blk < nblk - 1))(lambda: step(NA_QROWS))
    pl.when(blk == nblk - 1)(lambda: step(rows - NA_QROWS))

    for p in range(NA_HEADS // 2):
        cols = slice(p * lw, (p + 1) * lw)
        o_ref[:, cols] = _head_pair_rms(o_scr[:, cols], go_ref[:, cols], ones_bd).astype(o_ref.dtype)


def natten(proj, wb, go, *, layer, batch, col0):
    T = proj.shape[0]
    S = T // batch
    rows = S // GRID_W
    nq = NA_QROWS * GRID_W
    nblk = S // nq
    width = NA_HEADS * NA_HD
    cq = col0 // width

    def prev(i):
        return jnp.maximum(i - 1, 0)

    def nxt(i):
        return jnp.minimum(i + 1, nblk - 1)

    in_specs = [
        pl.BlockSpec((nq, width), lambda b, i: (b * nblk + i, cq)),
        pl.BlockSpec((nq, width), lambda b, i: (b * nblk + prev(i), cq + 1)),
        pl.BlockSpec((nq, width), lambda b, i: (b * nblk + i, cq + 1)),
        pl.BlockSpec((nq, width), lambda b, i: (b * nblk + nxt(i), cq + 1)),
        pl.BlockSpec((nq, width), lambda b, i: (b * nblk + prev(i), cq + 2)),
        pl.BlockSpec((nq, width), lambda b, i: (b * nblk + i, cq + 2)),
        pl.BlockSpec((nq, width), lambda b, i: (b * nblk + nxt(i), cq + 2)),
        pl.BlockSpec((None,) + wb.shape[1:], lambda b, i: (layer, 0, 0, 0, 0)),
        pl.BlockSpec((1, width), lambda b, i: (0, 0)),
    ]
    return pl.pallas_call(
        functools.partial(_natten_kernel, rows=rows),
        grid=(batch, nblk),
        in_specs=in_specs,
        out_specs=pl.BlockSpec((nq, width), lambda b, i: (b * nblk + i, 0)),
        out_shape=jax.ShapeDtypeStruct((T, width), BF16),
        scratch_shapes=[pltpu.VMEM((nq, width), F32)],
        compiler_params=_cparams(("parallel", "arbitrary")),
        name="natten",
    )(proj, proj, proj, proj, proj, proj, proj, wb, go)


def _mem_kv_kernel(mem_ref, g_ref, w_ref, gk_ref, k_ref, v_ref):
    mn = _rms(mem_ref[...], g_ref[...]).astype(BF16)
    kv = jnp.dot(mn, w_ref[...], preferred_element_type=F32)
    width = k_ref.shape[1]
    for h in range(MEM_HEADS):
        cols = slice(h * MEM_HD, (h + 1) * MEM_HD)
        k_ref[:, cols] = _rms(kv[:, cols], gk_ref[...]).astype(BF16)
    v_ref[...] = kv[:, width:].astype(BF16)


def mem_kv(mem2d, gain, wkv, gk, *, layer):
    n, D = mem2d.shape
    width = wkv.shape[2] // 2
    return pl.pallas_call(
        _mem_kv_kernel,
        grid=(1,),
        in_specs=[
            pl.BlockSpec((n, D), lambda i: (0, 0)),
            pl.BlockSpec((1, D), lambda i: (0, 0)),
            pl.BlockSpec((None, D, 2 * width), lambda i: (layer, 0, 0)),
            pl.BlockSpec((1, MEM_HD), lambda i: (0, 0)),
        ],
        out_specs=[pl.BlockSpec((n, width), lambda i: (0, 0)), pl.BlockSpec((n, width), lambda i: (0, 0))],
        out_shape=[jax.ShapeDtypeStruct((n, width), BF16), jax.ShapeDtypeStruct((n, width), BF16)],
        compiler_params=_cparams(("arbitrary",)),
        name="mem_kv",
    )(mem2d, gain, wkv, gk)


def _mem_attn_kernel(q_ref, k_ref, v_ref, gq_ref, go_ref, o_ref):
    for h in range(MEM_HEADS):
        cols = slice(h * MEM_HD, (h + 1) * MEM_HD)
        q = _rms(q_ref[:, cols].astype(F32), gq_ref[...]) * (MEM_HD ** -0.5)
        s = lax.dot_general(q.astype(BF16), k_ref[:, cols], NT_DIMS, preferred_element_type=F32)
        p = jnp.exp(s - jnp.max(s, axis=-1, keepdims=True))
        l = jnp.sum(p, axis=-1, keepdims=True)
        o = jnp.dot(p.astype(BF16), v_ref[:, cols], preferred_element_type=F32) / l
        o_ref[:, cols] = _rms(o, go_ref[:, cols]).astype(o_ref.dtype)


def mem_attn(proj, k_m, v_m, gq, go, *, batch, tm, col0):
    T = proj.shape[0]
    S = T // batch
    nm = k_m.shape[0] // batch
    width = MEM_HEADS * MEM_HD
    per_b = S // tm
    return pl.pallas_call(
        _mem_attn_kernel,
        grid=(T // tm,),
        in_specs=[
            pl.BlockSpec((tm, width), lambda i: (i, col0 // width)),
            pl.BlockSpec((nm, width), lambda i: (i // per_b, 0)),
            pl.BlockSpec((nm, width), lambda i: (i // per_b, 0)),
            pl.BlockSpec((1, MEM_HD), lambda i: (0, 0)),
            pl.BlockSpec((1, width), lambda i: (0, 0)),
        ],
        out_specs=pl.BlockSpec((tm, width), lambda i: (i, 0)),
        out_shape=jax.ShapeDtypeStruct((T, width), BF16),
        compiler_params=_cparams(("parallel",)),
        name="mem_attn",
    )(proj, k_m, v_m, gq, go)


def _out_proj_kernel(x_ref, ya_ref, yb_ref, yc_ref, w_ref, g_ref, o_ref, h_ref, *, sub):
    tm = x_ref.shape[0]
    wa, wb = ya_ref.shape[1], yb_ref.shape[1]
    for s in range(tm // sub):
        rows = slice(s * sub, (s + 1) * sub)
        acc = x_ref[rows, :] + jnp.dot(ya_ref[rows, :], w_ref[0:wa, :], preferred_element_type=F32)
        acc += jnp.dot(yb_ref[rows, :], w_ref[wa:wa + wb, :], preferred_element_type=F32)
        acc += jnp.dot(yc_ref[rows, :], w_ref[wa + wb:, :], preferred_element_type=F32)
        o_ref[rows, :] = acc
        h_ref[rows, :] = _rms(acc, g_ref[...]).astype(BF16)


def out_proj(x, y_gla, y_na, y_mem, w_out, gain, *, layer, tm, sub):
    T, D = x.shape
    wa, wb, wc = y_gla.shape[1], y_na.shape[1], y_mem.shape[1]
    return pl.pallas_call(
        functools.partial(_out_proj_kernel, sub=sub),
        grid=(T // tm,),
        in_specs=[
            pl.BlockSpec((tm, D), lambda i: (i, 0)),
            pl.BlockSpec((tm, wa), lambda i: (i, 0)),
            pl.BlockSpec((tm, wb), lambda i: (i, 0)),
            pl.BlockSpec((tm, wc), lambda i: (i, 0)),
            pl.BlockSpec((None, wa + wb + wc, D), lambda i: (layer, 0, 0), pipeline_mode=pl.Buffered(1)),
            pl.BlockSpec((1, D), lambda i: (0, 0)),
        ],
        out_specs=[pl.BlockSpec((tm, D), lambda i: (i, 0)), pl.BlockSpec((tm, D), lambda i: (i, 0))],
        out_shape=[jax.ShapeDtypeStruct((T, D), F32), jax.ShapeDtypeStruct((T, D), BF16)],
        compiler_params=_cparams(("parallel",)),
        name="out_proj",
    )(x, y_gla, y_na, y_mem, w_out, gain)


def _ffn_kernel(x_ref, h_ref, w1_ref, w3_ref, w2_ref, o_ref):
    @pl.when(pl.program_id(1) == 0)
    def _():
        o_ref[...] = x_ref[...]

    h = h_ref[...]
    gate = jnp.dot(h, w1_ref[...], preferred_element_type=F32)
    up = jnp.dot(h, w3_ref[...], preferred_element_type=F32)
    act = (gate * jax.nn.sigmoid(gate) * up).astype(BF16)
    o_ref[...] += jnp.dot(act, w2_ref[...], preferred_element_type=F32)


def ffn(x, h, w13, w2, *, layer, tm, tf):
    T, D = x.shape
    F = w2.shape[1]
    nf = F // tf
    return pl.pallas_call(
        _ffn_kernel,
        grid=(T // tm, nf),
        in_specs=[
            pl.BlockSpec((tm, D), lambda i, f: (i, 0)),
            pl.BlockSpec((tm, D), lambda i, f: (i, 0)),
            pl.BlockSpec((None, D, tf), lambda i, f: (layer, 0, f)),
            pl.BlockSpec((None, D, tf), lambda i, f: (layer, 0, nf + f)),
            pl.BlockSpec((None, tf, D), lambda i, f: (layer, f, 0)),
        ],
        out_specs=pl.BlockSpec((tm, D), lambda i, f: (i, 0)),
        out_shape=jax.ShapeDtypeStruct((T, D), F32),
        compiler_params=_cparams(("parallel", "arbitrary")),
        name="ffn",
    )(x, h, w13, w13, w2)


IN_PROJ_TN = 1024


def kernel(x, mem, attn_norm, w_in, gla_wg2_f, gla_bg_f, gla_wg2_b, gla_bg_b, gla_out_norm,
           na_q_norm, na_k_norm, na_rpb, na_out_norm, mem_norm, mem_wkv, mem_q_norm, mem_k_norm,
           mem_out_norm, w_out, ffn_norm, ffn_w13, ffn_w2):
    B, S, D = x.shape
    depth = w_in.shape[0]
    T = B * S
    gla_dk = GLA_HEADS * GLA_HK
    gla_dv = GLA_HEADS * GLA_HV
    na_w = NA_HEADS * NA_HD
    mem_w = MEM_HEADS * MEM_HD
    gate0 = 2 * gla_dk + 2 * gla_dv
    gate1 = gate0 + 2 * GLA_RANK
    na_col0 = gate0
    mem_col0 = gate0 + 3 * na_w
    assert w_in.shape[2] - 2 * GLA_RANK == mem_col0 + mem_w and gla_dv + na_w + mem_w == w_out.shape[1]
    assert na_col0 % IN_PROJ_TN == 0 and 2 * na_w == IN_PROJ_TN

    w_main = w_main_from_w_in(w_in, gate0=gate0, ngate=gate1 - gate0, tr=D, tn=IN_PROJ_TN)
    w_gate = jnp.pad(w_in[:, :, gate0:gate1], ((0, 0), (0, 0), (0, 128 - 2 * GLA_RANK))).astype(BF16)
    w_out_b = w_out.astype(BF16)
    w13_b = ffn_w13.astype(BF16)
    w2_b = ffn_w2.astype(BF16)
    wkv_b = mem_wkv.astype(BF16)
    gqk = jnp.concatenate([jnp.tile(na_q_norm * (NA_HD ** -0.5), (1, NA_HEADS)),
                           jnp.tile(na_k_norm, (1, NA_HEADS))], axis=1)
    wb = natten_bias_tables(na_rpb)

    xf = x.reshape(T, D)
    mem2d = mem.reshape(B * mem.shape[1], D)
    for l in range(depth):
        proj, gates = in_proj(xf, attn_norm[l][None], w_main, w_gate, gqk[l][None], layer=l,
                              tm=512, sub=256, tn=IN_PROJ_TN, qk_tile=na_col0 // IN_PROJ_TN)
        o_f = gla_direction(proj, gates, gla_wg2_f[l], gla_bg_f[l][None], batch=B, tb=1024, reverse=False)
        y_gla = gla_direction(proj, gates, gla_wg2_b[l], gla_bg_b[l][None], batch=B, tb=1024, reverse=True,
                              o_fwd=o_f, out_gain=gla_out_norm[l][None])
        y_na = natten(proj, wb, na_out_norm[l][None], layer=l, batch=B, col0=na_col0)
        k_m, v_m = mem_kv(mem2d, mem_norm[l][None], wkv_b, mem_k_norm[l][None], layer=l)
        y_mem = mem_attn(proj, k_m, v_m, mem_q_norm[l][None], mem_out_norm[l][None],
                         batch=B, tm=512, col0=mem_col0)
        xf, hf = out_proj(xf, y_gla, y_na, y_mem, w_out_b, ffn_norm[l][None], layer=l, tm=512, sub=256)
        xf = ffn(xf, hf, w13_b, w2_b, layer=l, tm=512, tf=512)
    return xf.reshape(B, S, D)
```

```python
import functools

import jax
import jax.numpy as jnp
from jax import lax
from jax.experimental import pallas as pl
from jax.experimental.pallas import tpu as pltpu

F32 = jnp.float32
BF16 = jnp.bfloat16

RMS_EPS = 1e-6
MASK_VALUE = -1e30

GRID_W = 64
GLA_HEADS = 4
GLA_HK = 128
GLA_HV = 256
GLA_RANK = 16
GLA_TAU = 16.0
GLA_CHUNK = 64
NA_HD = 64
NA_HEADS = 8
NA_ROWS = 8
NA_COLS = 16
MEM_HEADS = 4
MEM_HD = 128

V7X_VMEM_BYTES = 64 * 1024 * 1024
VMEM_LIMIT_BYTES = V7X_VMEM_BYTES - 8 * 1024 * 1024

NT_DIMS = (((1,), (1,)), ((), ()))


def _cparams(sem):
    return pltpu.CompilerParams(dimension_semantics=sem, vmem_limit_bytes=VMEM_LIMIT_BYTES)


def _rms(x, gain):
    ms = jnp.mean(x * x, axis=-1, keepdims=True)
    return (x * lax.rsqrt(ms + RMS_EPS)) * gain


def _head_pair_ones():
    r = lax.broadcasted_iota(jnp.int32, (2 * NA_HD, 2 * NA_HD), 0) // NA_HD
    c = lax.broadcasted_iota(jnp.int32, (2 * NA_HD, 2 * NA_HD), 1) // NA_HD
    return jnp.where(r == c, 1.0, 0.0).astype(BF16)


def _head_pair_rms(x, gain, ones_bd):
    ms = jnp.dot((x * x).astype(BF16), ones_bd, preferred_element_type=F32) * (1.0 / NA_HD)
    return (x * lax.rsqrt(ms + RMS_EPS)) * gain


def _mem_attention(mq, km_ref, vm_ref, gq_ref, go_ref, y_ref, rows):
    heads = range(MEM_HEADS)
    cols = [slice(h * MEM_HD, (h + 1) * MEM_HD) for h in heads]
    q = [(_rms(mq[:, c], gq_ref[...]) * (MEM_HD ** -0.5)).astype(BF16) for c in cols]
    s = [lax.dot_general(q[h], km_ref[:, cols[h]], NT_DIMS, preferred_element_type=F32) for h in heads]
    p = [jnp.exp(sh - jnp.max(sh, axis=-1, keepdims=True)) for sh in s]
    l = [jnp.sum(ph, axis=-1, keepdims=True) for ph in p]
    o = [jnp.dot(p[h].astype(BF16), vm_ref[:, cols[h]], preferred_element_type=F32) / l[h] for h in heads]
    for h in heads:
        y_ref[rows, cols[h]] = _rms(o[h], go_ref[:, cols[h]]).astype(y_ref.dtype)


def _in_proj_kernel(x_ref, g_ref, w_ref, wg_ref, gqk_ref, km_ref, vm_ref, gmq_ref, gmo_ref,
                    proj_ref, gate_ref, ymem_ref, *, sub, tn, qk_tile):
    tm = x_ref.shape[0]
    ones_bd = _head_pair_ones()
    lw = 2 * NA_HD
    n_proj = proj_ref.shape[1]
    for s in range(tm // sub):
        rows = slice(s * sub, (s + 1) * sub)
        xn = _rms(x_ref[rows, :], g_ref[...]).astype(BF16)
        gate_ref[rows, :] = jnp.dot(xn, wg_ref[...], preferred_element_type=F32)
        ntile = w_ref.shape[1] // tn
        for n in [ntile - 1] + list(range(ntile - 1)):
            acc = jnp.dot(xn, w_ref[:, n * tn:(n + 1) * tn], preferred_element_type=F32)
            keep = min(tn, n_proj - n * tn)
            if n == qk_tile:
                for c in range(tn // lw):
                    cols = slice(c * lw, (c + 1) * lw)
                    proj_ref[rows, n * tn + c * lw:n * tn + (c + 1) * lw] = _head_pair_rms(
                        acc[:, cols], gqk_ref[:, cols], ones_bd).astype(BF16)
            else:
                proj_ref[rows, n * tn:n * tn + keep] = acc[:, :keep].astype(BF16)
            if keep < tn:
                _mem_attention(acc[:, keep:], km_ref, vm_ref, gmq_ref, gmo_ref, ymem_ref, rows)


def in_proj(x, gain, w_main, w_gate, gqk, k_m, v_m, gmq, gmo, *, layer, batch, tm, sub, tn, qk_tile):
    T, D = x.shape
    N = w_main.shape[2]
    G = w_gate.shape[2]
    mem_w = MEM_HEADS * MEM_HD
    n_mem = k_m.shape[1] // batch
    per_b = (T // batch) // tm
    return pl.pallas_call(
        functools.partial(_in_proj_kernel, sub=sub, tn=tn, qk_tile=qk_tile),
        grid=(T // tm,),
        in_specs=[
            pl.BlockSpec((tm, D), lambda i: (i, 0)),
            pl.BlockSpec((1, D), lambda i: (0, 0)),
            pl.BlockSpec((None, D, N), lambda i: (layer, 0, 0), pipeline_mode=pl.Buffered(1)),
            pl.BlockSpec((None, D, G), lambda i: (layer, 0, 0), pipeline_mode=pl.Buffered(1)),
            pl.BlockSpec((1, tn), lambda i: (0, 0)),
            pl.BlockSpec((None, n_mem, mem_w), lambda i: (layer, i // per_b, 0)),
            pl.BlockSpec((None, n_mem, mem_w), lambda i: (layer, i // per_b, 0)),
            pl.BlockSpec((1, MEM_HD), lambda i: (0, 0)),
            pl.BlockSpec((1, mem_w), lambda i: (0, 0)),
        ],
        out_specs=[
            pl.BlockSpec((tm, N - mem_w), lambda i: (i, 0)),
            pl.BlockSpec((tm, G), lambda i: (i, 0)),
            pl.BlockSpec((tm, mem_w), lambda i: (i, 0)),
        ],
        out_shape=[jax.ShapeDtypeStruct((T, N - mem_w), BF16), jax.ShapeDtypeStruct((T, G), F32),
                   jax.ShapeDtypeStruct((T, mem_w), BF16)],
        compiler_params=_cparams(("parallel",)),
        name="in_proj",
    )(x, gain, w_main, w_gate, gqk, k_m, v_m, gmq, gmo)


def _log_sigmoid(z):
    return jnp.minimum(z, 0.0) - jnp.log(1.0 + jnp.exp(-jnp.abs(z)))


def _gla_kernel(q_ref, k_ref, v_ref, gate_ref, wg_ref, bg_ref, *rest, reverse, final, gate_off):
    if final:
        r_ref, of_ref, gn_ref, o_ref, st_ref = rest
    else:
        o_ref, st_ref = rest
    C = GLA_CHUNK
    H, dk, dv = GLA_HEADS, GLA_HK, GLA_HV
    tb = q_ref.shape[0]
    nchunk = tb // C

    @pl.when(pl.program_id(1) == 0)
    def _():
        st_ref[...] = jnp.zeros_like(st_ref)

    g = gate_ref[:, gate_off:gate_off + GLA_RANK]
    z = jnp.dot(g.astype(BF16), wg_ref[...].astype(BF16), preferred_element_type=F32) + bg_ref[...]
    la = _log_sigmoid(z) * (1.0 / GLA_TAU)

    la_t = jnp.concatenate([la[c * C:(c + 1) * C, :] for c in range(nchunk)], axis=1)
    row = lax.broadcasted_iota(jnp.int32, (C, C), 0)
    col = lax.broadcasted_iota(jnp.int32, (C, C), 1)
    if reverse:
        tri = (col >= row)
        keep = col > row
    else:
        tri = (col <= row)
        keep = col <= row
    tri = jnp.where(tri, 1.0, 0.0).astype(BF16)
    la_hi = la_t.astype(BF16)
    la_lo = (la_t - la_hi.astype(F32)).astype(BF16)
    cum = (jnp.dot(tri, la_hi, preferred_element_type=F32)
           + jnp.dot(tri, la_lo, preferred_element_type=F32))

    order = range(nchunk - 1, -1, -1) if reverse else range(nchunk)
    heads = range(H)
    kcols = [slice(h * dk, (h + 1) * dk) for h in heads]
    vcols = [slice(h * dv, (h + 1) * dv) for h in heads]
    for c in order:
        rows = slice(c * C, (c + 1) * C)
        cum_c = cum[:, c * H * dk:(c + 1) * H * dk]
        last = cum_c[0:1, :] if reverse else cum_c[C - 1:C, :]
        qc = q_ref[rows, :].astype(F32) * (GLA_HK ** -0.5)
        kc = k_ref[rows, :].astype(F32)
        q_e = (qc * jnp.exp(cum_c)).astype(BF16)
        k_e = (kc * jnp.exp(-cum_c)).astype(BF16)
        k_end = (kc * jnp.exp(last - cum_c)).astype(BF16)
        decay = jnp.exp(last)
        vc = [v_ref[rows, vcols[h]] for h in heads]
        sc = [lax.dot_general(q_e[:, kcols[h]], k_e[:, kcols[h]], NT_DIMS, preferred_element_type=F32)
              for h in heads]
        sc = [jnp.where(keep, s, 0.0).astype(BF16) for s in sc]
        st = [st_ref[h] for h in heads]
        o = [jnp.dot(sc[h], vc[h], preferred_element_type=F32)
             + lax.dot_general(q_e[:, kcols[h]], st[h].astype(BF16), NT_DIMS, preferred_element_type=F32)
             for h in heads]
        for h in heads:
            st_ref[h] = (st[h] * decay[:, kcols[h]]
                         + jnp.dot(vc[h].T, k_end[:, kcols[h]], preferred_element_type=F32))
        for h in heads:
            if final:
                oh = o[h] + of_ref[rows, vcols[h]].astype(F32)
                y = _rms(oh, gn_ref[:, vcols[h]])
                r = r_ref[rows, vcols[h]].astype(F32)
                o_ref[rows, vcols[h]] = (y * (r * jax.nn.sigmoid(r))).astype(o_ref.dtype)
            else:
                o_ref[rows, vcols[h]] = o[h].astype(o_ref.dtype)


def gla_direction(proj, gates, wg2, bg, *, batch, tb, reverse, o_fwd=None, out_gain=None):
    T = proj.shape[0]
    S = T // batch
    nb = S // tb
    final = o_fwd is not None
    H, dk, dv = GLA_HEADS, GLA_HK, GLA_HV
    wk, wv = H * dk, H * dv

    def tok(b, i):
        return b * nb + ((nb - 1 - i) if reverse else i)

    in_specs = [
        pl.BlockSpec((tb, wk), lambda b, i: (tok(b, i), 0)),
        pl.BlockSpec((tb, wk), lambda b, i: (tok(b, i), 1)),
        pl.BlockSpec((tb, wv), lambda b, i: (tok(b, i), (2 * wk) // wv)),
        pl.BlockSpec((tb, gates.shape[1]), lambda b, i: (tok(b, i), 0)),
        pl.BlockSpec((GLA_RANK, wk), lambda b, i: (0, 0)),
        pl.BlockSpec((1, wk), lambda b, i: (0, 0)),
    ]
    args = [proj, proj, proj, gates, wg2, bg]
    if final:
        in_specs += [
            pl.BlockSpec((tb, wv), lambda b, i: (tok(b, i), (2 * wk) // wv + 1)),
            pl.BlockSpec((tb, wv), lambda b, i: (tok(b, i), 0)),
            pl.BlockSpec((1, wv), lambda b, i: (0, 0)),
        ]
        args += [proj, o_fwd, out_gain]
    kern = functools.partial(_gla_kernel, reverse=reverse, final=final,
                             gate_off=GLA_RANK if reverse else 0)
    return pl.pallas_call(
        kern,
        grid=(batch, nb),
        in_specs=in_specs,
        out_specs=pl.BlockSpec((tb, wv), lambda b, i: (tok(b, i), 0)),
        out_shape=jax.ShapeDtypeStruct((T, wv), BF16),
        scratch_shapes=[pltpu.VMEM((H, dv, dk), F32)],
        compiler_params=_cparams(("parallel", "arbitrary")),
        name="gla_bwd" if reverse else "gla_fwd",
    )(*args)


NA_QROWS = 8
NA_DR = 2 * NA_ROWS - 1
NA_DC = 2 * NA_COLS - 1


def _natten_bias_kernel(rpb_ref, o_ref):
    W = GRID_W
    base = (pl.program_id(0) * NA_HEADS + pl.program_id(1)) * (NA_DR * NA_DC)
    qc = lax.broadcasted_iota(jnp.int32, (W, 2 * W), 0)
    ln = lax.broadcasted_iota(jnp.int32, (W, 2 * W), 1)
    kc = ln & (W - 1)
    dc_idx = jnp.clip(kc - qc, -(NA_COLS - 1), NA_COLS - 1) + (NA_COLS - 1)
    cs = jnp.clip(qc - NA_COLS // 2, 0, W - NA_COLS)
    in_win = (kc >= cs) & (kc < cs + NA_COLS)
    left = ln < W
    toeplitz = []
    for dr in range(NA_DR):
        t = jnp.full((W, 2 * W), MASK_VALUE, F32)
        for dc in range(NA_DC):
            t = jnp.where(dc_idx == dc, rpb_ref[base + dr * NA_DC + dc], t)
        toeplitz.append(jnp.where(in_win, t, MASK_VALUE))
    for dr0 in range(NA_ROWS):
        for jp in range(NA_ROWS // 2):
            o_ref[dr0, :, jp * 2 * W:(jp + 1) * 2 * W] = jnp.where(
                left, toeplitz[dr0 + 2 * jp], toeplitz[dr0 + 2 * jp + 1])


def natten_bias_tables(rpb):
    L, H = rpb.shape[:2]
    W = GRID_W
    return pl.pallas_call(
        _natten_bias_kernel,
        grid=(L, H),
        in_specs=[pl.BlockSpec(memory_space=pltpu.SMEM)],
        out_specs=pl.BlockSpec((None, None, NA_ROWS, W, NA_ROWS * W), lambda l, h: (l, h, 0, 0, 0)),
        out_shape=jax.ShapeDtypeStruct((L, H, NA_ROWS, W, NA_ROWS * W), F32),
        compiler_params=_cparams(("parallel", "parallel")),
        name="natten_bias",
    )(rpb.reshape(-1))


def _natten_kernel(q_ref, kp_ref, kc_ref, kn_ref, vp_ref, vc_ref, vn_ref, wb_ref, go_ref, o_ref,
                   o_scr, *, rows):
    W = GRID_W
    nq = q_ref.shape[0]
    lw = 2 * NA_HD
    nwin = NA_ROWS * W

    def window(refs, start, cols):
        parts = []
        for b, ref in enumerate(refs):
            lo, hi = max(start, b * nq), min(start + nwin, (b + 1) * nq)
            if lo < hi:
                parts.append(ref[lo - b * nq:hi - b * nq, cols])
        return parts[0] if len(parts) == 1 else jnp.concatenate(parts, axis=0)

    k_refs = (kp_ref, kc_ref, kn_ref)
    v_refs = (vp_ref, vc_ref, vn_ref)
    head0 = lax.broadcasted_iota(jnp.int32, (1, lw), 1) < NA_HD
    ones_bd = _head_pair_ones()
    blk = pl.program_id(1)
    nblk = rows // NA_QROWS

    def step(r0):
        for a in range(NA_QROWS):
            rs = min(max(r0 + a - NA_ROWS // 2, 0), rows - NA_ROWS)
            off = rs - (r0 - NA_QROWS)
            dr0 = rs - (r0 + a) + (NA_ROWS - 1)
            qrows = slice(a * W, (a + 1) * W)
            pairs = range(NA_HEADS // 2)
            cols = [slice(p * lw, (p + 1) * lw) for p in pairs]
            s = []
            for p in pairs:
                qa = q_ref[qrows, cols[p]]
                zero = jnp.zeros_like(qa)
                lhs = jnp.concatenate([jnp.where(head0, qa, zero), jnp.where(head0, zero, qa)], axis=0)
                sp = lax.dot_general(lhs, window(k_refs, off * W, cols[p]), NT_DIMS,
                                     preferred_element_type=F32)
                s.append(sp + jnp.concatenate([wb_ref[2 * p, dr0], wb_ref[2 * p + 1, dr0]], axis=0))
            m = [jnp.max(sp, axis=-1, keepdims=True) for sp in s]
            e = [jnp.exp(sp - mp) for sp, mp in zip(s, m)]
            l = [jnp.sum(ep, axis=-1, keepdims=True) for ep in e]
            r = [jnp.dot(e[p].astype(BF16), window(v_refs, off * W, cols[p]), preferred_element_type=F32)
                 for p in pairs]
            for p in pairs:
                rp = r[p] / l[p]
                o_scr[qrows, cols[p]] = jnp.where(head0, rp[0:W], rp[W:2 * W])

    pl.when(blk == 0)(lambda: step(0))
    pl.when(jnp.logical_and(blk > 0, blk < nblk - 1))(lambda: step(NA_QROWS))
    pl.when(blk == nblk - 1)(lambda: step(rows - NA_QROWS))

    for p in range(NA_HEADS // 2):
        cols = slice(p * lw, (p + 1) * lw)
        o_ref[:, cols] = _head_pair_rms(o_scr[:, cols], go_ref[:, cols], ones_bd).astype(o_ref.dtype)


def natten(proj, wb, go, *, layer, batch, col0):
    T = proj.shape[0]
    S = T // batch
    rows = S // GRID_W
    nq = NA_QROWS * GRID_W
    nblk = S // nq
    width = NA_HEADS * NA_HD
    cq = col0 // width

    def prev(i):
        return jnp.maximum(i - 1, 0)

    def nxt(i):
        return jnp.minimum(i + 1, nblk - 1)

    in_specs = [
        pl.BlockSpec((nq, width), lambda b, i: (b * nblk + i, cq)),
        pl.BlockSpec((nq, width), lambda b, i: (b * nblk + prev(i), cq + 1)),
        pl.BlockSpec((nq, width), lambda b, i: (b * nblk + i, cq + 1)),
        pl.BlockSpec((nq, width), lambda b, i: (b * nblk + nxt(i), cq + 1)),
        pl.BlockSpec((nq, width), lambda b, i: (b * nblk + prev(i), cq + 2)),
        pl.BlockSpec((nq, width), lambda b, i: (b * nblk + i, cq + 2)),
        pl.BlockSpec((nq, width), lambda b, i: (b * nblk + nxt(i), cq + 2)),
        pl.BlockSpec((None,) + wb.shape[1:], lambda b, i: (layer, 0, 0, 0, 0)),
        pl.BlockSpec((1, width), lambda b, i: (0, 0)),
    ]
    return pl.pallas_call(
        functools.partial(_natten_kernel, rows=rows),
        grid=(batch, nblk),
        in_specs=in_specs,
        out_specs=pl.BlockSpec((nq, width), lambda b, i: (b * nblk + i, 0)),
        out_shape=jax.ShapeDtypeStruct((T, width), BF16),
        scratch_shapes=[pltpu.VMEM((nq, width), F32)],
        compiler_params=_cparams(("parallel", "arbitrary")),
        name="natten",
    )(proj, proj, proj, proj, proj, proj, proj, wb, go)


def _mem_kv_kernel(mem_ref, g_ref, w_ref, gk_ref, k_ref, v_ref):
    mn = _rms(mem_ref[...], g_ref[...]).astype(BF16)
    kv = jnp.dot(mn, w_ref[...], preferred_element_type=F32)
    width = k_ref.shape[1]
    for h in range(MEM_HEADS):
        cols = slice(h * MEM_HD, (h + 1) * MEM_HD)
        k_ref[:, cols] = _rms(kv[:, cols], gk_ref[...]).astype(BF16)
    v_ref[...] = kv[:, width:].astype(BF16)


def mem_kv(mem2d, gain, wkv, gk):
    n, D = mem2d.shape
    L = wkv.shape[0]
    width = wkv.shape[2] // 2
    return pl.pallas_call(
        _mem_kv_kernel,
        grid=(L,),
        in_specs=[
            pl.BlockSpec((n, D), lambda l: (0, 0)),
            pl.BlockSpec((None, 1, D), lambda l: (l, 0, 0)),
            pl.BlockSpec((None, D, 2 * width), lambda l: (l, 0, 0)),
            pl.BlockSpec((None, 1, MEM_HD), lambda l: (l, 0, 0)),
        ],
        out_specs=[pl.BlockSpec((None, n, width), lambda l: (l, 0, 0)),
                   pl.BlockSpec((None, n, width), lambda l: (l, 0, 0))],
        out_shape=[jax.ShapeDtypeStruct((L, n, width), BF16), jax.ShapeDtypeStruct((L, n, width), BF16)],
        compiler_params=_cparams(("parallel",)),
        name="mem_kv",
    )(mem2d, gain, wkv, gk)


def _out_proj_kernel(x_ref, ya_ref, yb_ref, yc_ref, w_ref, g_ref, o_ref, h_ref, *, sub):
    tm = x_ref.shape[0]
    wa, wb = ya_ref.shape[1], yb_ref.shape[1]
    for s in range(tm // sub):
        rows = slice(s * sub, (s + 1) * sub)
        acc = x_ref[rows, :] + jnp.dot(ya_ref[rows, :], w_ref[0:wa, :], preferred_element_type=F32)
        acc += jnp.dot(yb_ref[rows, :], w_ref[wa:wa + wb, :], preferred_element_type=F32)
        acc += jnp.dot(yc_ref[rows, :], w_ref[wa + wb:, :], preferred_element_type=F32)
        o_ref[rows, :] = acc
        h_ref[rows, :] = _rms(acc, g_ref[...]).astype(BF16)


def out_proj(x, y_gla, y_na, y_mem, w_out, gain, *, layer, tm, sub):
    T, D = x.shape
    wa, wb, wc = y_gla.shape[1], y_na.shape[1], y_mem.shape[1]
    return pl.pallas_call(
        functools.partial(_out_proj_kernel, sub=sub),
        grid=(T // tm,),
        in_specs=[
            pl.BlockSpec((tm, D), lambda i: (i, 0)),
            pl.BlockSpec((tm, wa), lambda i: (i, 0)),
            pl.BlockSpec((tm, wb), lambda i: (i, 0)),
            pl.BlockSpec((tm, wc), lambda i: (i, 0)),
            pl.BlockSpec((None, wa + wb + wc, D), lambda i: (layer, 0, 0), pipeline_mode=pl.Buffered(1)),
            pl.BlockSpec((1, D), lambda i: (0, 0)),
        ],
        out_specs=[pl.BlockSpec((tm, D), lambda i: (i, 0)), pl.BlockSpec((tm, D), lambda i: (i, 0))],
        out_shape=[jax.ShapeDtypeStruct((T, D), F32), jax.ShapeDtypeStruct((T, D), BF16)],
        compiler_params=_cparams(("parallel",)),
        name="out_proj",
    )(x, y_gla, y_na, y_mem, w_out, gain)


def _ffn_kernel(x_ref, h_ref, w1_ref, w3_ref, w2_ref, o_ref):
    @pl.when(pl.program_id(1) == 0)
    def _():
        o_ref[...] = x_ref[...]

    h = h_ref[...]
    gate = jnp.dot(h, w1_ref[...], preferred_element_type=F32)
    up = jnp.dot(h, w3_ref[...], preferred_element_type=F32)
    act = (gate * jax.nn.sigmoid(gate) * up).astype(BF16)
    o_ref[...] += jnp.dot(act, w2_ref[...], preferred_element_type=F32)


def ffn(x, h, w13, w2, *, layer, tm, tf):
    T, D = x.shape
    F = w2.shape[1]
    nf = F // tf
    return pl.pallas_call(
        _ffn_kernel,
        grid=(T // tm, nf),
        in_specs=[
            pl.BlockSpec((tm, D), lambda i, f: (i, 0)),
            pl.BlockSpec((tm, D), lambda i, f: (i, 0)),
            pl.BlockSpec((None, D, tf), lambda i, f: (layer, 0, f)),
            pl.BlockSpec((None, D, tf), lambda i, f: (layer, 0, nf + f)),
            pl.BlockSpec((None, tf, D), lambda i, f: (layer, f, 0)),
        ],
        out_specs=pl.BlockSpec((tm, D), lambda i, f: (i, 0)),
        out_shape=jax.ShapeDtypeStruct((T, D), F32),
        compiler_params=_cparams(("parallel", "arbitrary")),
        name="ffn",
    )(x, h, w13, w13, w2)


IN_PROJ_TN = 1024


def kernel(x, mem, attn_norm, w_in, gla_wg2_f, gla_bg_f, gla_wg2_b, gla_bg_b, gla_out_norm,
           na_q_norm, na_k_norm, na_rpb, na_out_norm, mem_norm, mem_wkv, mem_q_norm, mem_k_norm,
           mem_out_norm, w_out, ffn_norm, ffn_w13, ffn_w2):
    B, S, D = x.shape
    depth = w_in.shape[0]
    T = B * S
    gla_dk = GLA_HEADS * GLA_HK
    gla_dv = GLA_HEADS * GLA_HV
    na_w = NA_HEADS * NA_HD
    mem_w = MEM_HEADS * MEM_HD
    gate0 = 2 * gla_dk + 2 * gla_dv
    gate1 = gate0 + 2 * GLA_RANK
    na_col0 = gate0
    mem_col0 = gate0 + 3 * na_w
    assert w_in.shape[2] - 2 * GLA_RANK == mem_col0 + mem_w and gla_dv + na_w + mem_w == w_out.shape[1]
    assert na_col0 % IN_PROJ_TN == 0 and 2 * na_w == IN_PROJ_TN

    w_in_b = w_in.astype(BF16)
    w_main = jnp.concatenate([w_in_b[:, :, :gate0], w_in_b[:, :, gate1:]], axis=2)
    w_gate = jnp.pad(w_in_b[:, :, gate0:gate1], ((0, 0), (0, 0), (0, 128 - 2 * GLA_RANK)))
    w_out_b = w_out.astype(BF16)
    w13_b = ffn_w13.astype(BF16)
    w2_b = ffn_w2.astype(BF16)
    wkv_b = mem_wkv.astype(BF16)
    gqk = jnp.concatenate([jnp.tile(na_q_norm * (NA_HD ** -0.5), (1, NA_HEADS)),
                           jnp.tile(na_k_norm, (1, NA_HEADS))], axis=1)
    wb = natten_bias_tables(na_rpb)

    xf = x.reshape(T, D)
    k_m, v_m = mem_kv(mem.reshape(B * mem.shape[1], D), mem_norm[:, None], wkv_b, mem_k_norm[:, None])
    for l in range(depth):
        proj, gates, y_mem = in_proj(xf, attn_norm[l][None], w_main, w_gate, gqk[l][None], k_m, v_m,
                                     mem_q_norm[l][None], mem_out_norm[l][None], layer=l, batch=B,
                                     tm=512, sub=256, tn=IN_PROJ_TN, qk_tile=na_col0 // IN_PROJ_TN)
        o_f = gla_direction(proj, gates, gla_wg2_f[l], gla_bg_f[l][None], batch=B, tb=1024, reverse=False)
        y_gla = gla_direction(proj, gates, gla_wg2_b[l], gla_bg_b[l][None], batch=B, tb=1024, reverse=True,
                              o_fwd=o_f, out_gain=gla_out_norm[l][None])
        y_na = natten(proj, wb, na_out_norm[l][None], layer=l, batch=B, col0=na_col0)
        xf, hf = out_proj(xf, y_gla, y_na, y_mem, w_out_b, ffn_norm[l][None], layer=l, tm=512, sub=256)
        xf = ffn(xf, hf, w13_b, w2_b, layer=l, tm=512, tf=512)
    return xf.reshape(B, S, D)
```

```python
import functools

import jax
import jax.numpy as jnp
from jax import lax
from jax.experimental import pallas as pl
from jax.experimental.pallas import tpu as pltpu

F32 = jnp.float32
BF16 = jnp.bfloat16

RMS_EPS = 1e-6
MASK_VALUE = -1e30

GRID_W = 64
GLA_HEADS = 4
GLA_HK = 128
GLA_HV = 256
GLA_RANK = 16
GLA_TAU = 16.0
GLA_CHUNK = 64
NA_HD = 64
NA_HEADS = 8
NA_ROWS = 8
NA_COLS = 16
MEM_HEADS = 4
MEM_HD = 128

V7X_MXU_WIDTH = 256
V7X_VMEM_BYTES = 64 * 1024 * 1024
VMEM_LIMIT_BYTES = V7X_VMEM_BYTES - 8 * 1024 * 1024

NT_DIMS = (((1,), (1,)), ((), ()))


def _cparams(sem):
    return pltpu.CompilerParams(dimension_semantics=sem, vmem_limit_bytes=VMEM_LIMIT_BYTES)


def _rms(x, gain):
    ms = jnp.mean(x * x, axis=-1, keepdims=True)
    return (x * lax.rsqrt(ms + RMS_EPS)) * gain


def _head_pair_ones(width=2 * NA_HD):
    r = lax.broadcasted_iota(jnp.int32, (width, width), 0) // NA_HD
    c = lax.broadcasted_iota(jnp.int32, (width, width), 1) // NA_HD
    return jnp.where(r == c, 1.0, 0.0).astype(BF16)


def _head_pair_rms(x, gain, ones_bd):
    ms = jnp.dot((x * x).astype(BF16), ones_bd, preferred_element_type=F32) * (1.0 / NA_HD)
    return (x * lax.rsqrt(ms + RMS_EPS)) * gain


def _mem_attention(mq, km_ref, vm_ref, gq_ref, go_ref, y_ref, rows):
    heads = range(MEM_HEADS)
    cols = [slice(h * MEM_HD, (h + 1) * MEM_HD) for h in heads]
    q = [(_rms(mq[:, c], gq_ref[...]) * (MEM_HD ** -0.5)).astype(BF16) for c in cols]
    s = [lax.dot_general(q[h], km_ref[:, cols[h]], NT_DIMS, preferred_element_type=F32) for h in heads]
    p = [jnp.exp(sh - jnp.max(sh, axis=-1, keepdims=True)) for sh in s]
    l = [jnp.sum(ph, axis=-1, keepdims=True) for ph in p]
    o = [jnp.dot(p[h].astype(BF16), vm_ref[:, cols[h]], preferred_element_type=F32) / l[h] for h in heads]
    for h in heads:
        y_ref[rows, cols[h]] = _rms(o[h], go_ref[:, cols[h]]).astype(y_ref.dtype)


def _in_proj_kernel(x_ref, g_ref, wt_ref, gqk_ref, km_ref, vm_ref, gmq_ref, gmo_ref,
                    proj_ref, gate_ref, ymem_ref, *, sub, tn, qk_tile, gate0):
    tm = x_ref.shape[0]
    ngate = gate_ref.shape[1]
    lw = V7X_MXU_WIDTH
    ones_bd = _head_pair_ones(lw)
    n_proj = proj_ref.shape[1]
    ntile = (wt_ref.shape[0] - ngate) // tn
    for s in range(tm // sub):
        rows = slice(s * sub, (s + 1) * sub)
        xn = _rms(x_ref[rows, :], g_ref[...]).astype(BF16)
        gate_ref[rows, :] = lax.dot_general(xn, wt_ref[gate0:gate0 + ngate, :], NT_DIMS,
                                            preferred_element_type=F32)
        for n in [ntile - 1] + list(range(ntile - 1)):
            w0 = n * tn + (ngate if n * tn >= gate0 else 0)
            acc = lax.dot_general(xn, wt_ref[w0:w0 + tn, :], NT_DIMS, preferred_element_type=F32)
            keep = min(tn, n_proj - n * tn)
            if n == qk_tile:
                for c in range(tn // lw):
                    cols = slice(c * lw, (c + 1) * lw)
                    proj_ref[rows, n * tn + c * lw:n * tn + (c + 1) * lw] = _head_pair_rms(
                        acc[:, cols], gqk_ref[:, cols], ones_bd).astype(BF16)
            else:
                proj_ref[rows, n * tn:n * tn + keep] = acc[:, :keep].astype(BF16)
            if keep < tn:
                _mem_attention(acc[:, keep:], km_ref, vm_ref, gmq_ref, gmo_ref, ymem_ref, rows)


def in_proj(x, gain, w_t, gqk, k_m, v_m, gmq, gmo, *, layer, batch, tm, sub, tn, qk_tile, gate0, ngate):
    T, D = x.shape
    n_in = w_t.shape[1]
    N = n_in - ngate
    G = ngate
    assert gate0 % tn == 0 and N % tn == 0
    mem_w = MEM_HEADS * MEM_HD
    n_mem = k_m.shape[1] // batch
    per_b = (T // batch) // tm
    return pl.pallas_call(
        functools.partial(_in_proj_kernel, sub=sub, tn=tn, qk_tile=qk_tile, gate0=gate0),
        grid=(T // tm,),
        in_specs=[
            pl.BlockSpec((tm, D), lambda i: (i, 0)),
            pl.BlockSpec((1, D), lambda i: (0, 0)),
            pl.BlockSpec((None, n_in, D), lambda i: (layer, 0, 0), pipeline_mode=pl.Buffered(1)),
            pl.BlockSpec((1, tn), lambda i: (0, 0)),
            pl.BlockSpec((None, n_mem, mem_w), lambda i: (layer, i // per_b, 0)),
            pl.BlockSpec((None, n_mem, mem_w), lambda i: (layer, i // per_b, 0)),
            pl.BlockSpec((1, MEM_HD), lambda i: (0, 0)),
            pl.BlockSpec((1, mem_w), lambda i: (0, 0)),
        ],
        out_specs=[
            pl.BlockSpec((tm, N - mem_w), lambda i: (i, 0)),
            pl.BlockSpec((tm, G), lambda i: (i, 0)),
            pl.BlockSpec((tm, mem_w), lambda i: (i, 0)),
        ],
        out_shape=[jax.ShapeDtypeStruct((T, N - mem_w), BF16), jax.ShapeDtypeStruct((T, G), F32),
                   jax.ShapeDtypeStruct((T, mem_w), BF16)],
        compiler_params=_cparams(("parallel",)),
        name="in_proj",
    )(x, gain, w_t, gqk, k_m, v_m, gmq, gmo)


def _log_sigmoid(z):
    return jnp.minimum(z, 0.0) - jnp.log(1.0 + jnp.exp(-jnp.abs(z)))


def _gla_kernel(q_ref, k_ref, v_ref, gate_ref, wg_ref, bg_ref, *rest, reverse, final, gate_off):
    if final:
        r_ref, of_ref, gn_ref, o_ref, st_ref = rest
    else:
        o_ref, st_ref = rest
    C = GLA_CHUNK
    H, dk, dv = GLA_HEADS, GLA_HK, GLA_HV
    tb = q_ref.shape[0]
    nchunk = tb // C

    @pl.when(pl.program_id(1) == 0)
    def _():
        st_ref[...] = jnp.zeros_like(st_ref)

    g = gate_ref[:, gate_off:gate_off + GLA_RANK]
    z = jnp.dot(g.astype(BF16), wg_ref[...].astype(BF16), preferred_element_type=F32) + bg_ref[...]
    la = _log_sigmoid(z) * (1.0 / GLA_TAU)

    la_t = jnp.concatenate([la[c * C:(c + 1) * C, :] for c in range(nchunk)], axis=1)
    row = lax.broadcasted_iota(jnp.int32, (C, C), 0)
    col = lax.broadcasted_iota(jnp.int32, (C, C), 1)
    if reverse:
        tri = (col >= row)
        keep = col > row
    else:
        tri = (col <= row)
        keep = col <= row
    tri = jnp.where(tri, 1.0, 0.0).astype(BF16)
    la_hi = la_t.astype(BF16)
    la_lo = (la_t - la_hi.astype(F32)).astype(BF16)
    cum = (jnp.dot(tri, la_hi, preferred_element_type=F32)
           + jnp.dot(tri, la_lo, preferred_element_type=F32))

    order = range(nchunk - 1, -1, -1) if reverse else range(nchunk)
    heads = range(H)
    kcols = [slice(h * dk, (h + 1) * dk) for h in heads]
    vcols = [slice(h * dv, (h + 1) * dv) for h in heads]
    for c in order:
        rows = slice(c * C, (c + 1) * C)
        cum_c = cum[:, c * H * dk:(c + 1) * H * dk]
        last = cum_c[0:1, :] if reverse else cum_c[C - 1:C, :]
        qc = q_ref[rows, :].astype(F32) * (GLA_HK ** -0.5)
        kc = k_ref[rows, :].astype(F32)
        q_e = (qc * jnp.exp(cum_c)).astype(BF16)
        k_e = (kc * jnp.exp(-cum_c)).astype(BF16)
        k_end = (kc * jnp.exp(last - cum_c)).astype(BF16)
        decay = jnp.exp(last)
        vc = [v_ref[rows, vcols[h]] for h in heads]
        sc = [lax.dot_general(q_e[:, kcols[h]], k_e[:, kcols[h]], NT_DIMS, preferred_element_type=F32)
              for h in heads]
        sc = [jnp.where(keep, s, 0.0).astype(BF16) for s in sc]
        st = [st_ref[h] for h in heads]
        o = [jnp.dot(sc[h], vc[h], preferred_element_type=F32)
             + lax.dot_general(q_e[:, kcols[h]], st[h].astype(BF16), NT_DIMS, preferred_element_type=F32)
             for h in heads]
        for h in heads:
            st_ref[h] = (st[h] * decay[:, kcols[h]]
                         + jnp.dot(vc[h].T, k_end[:, kcols[h]], preferred_element_type=F32))
        for h in heads:
            if final:
                oh = o[h] + of_ref[rows, vcols[h]].astype(F32)
                y = _rms(oh, gn_ref[:, vcols[h]])
                r = r_ref[rows, vcols[h]].astype(F32)
                o_ref[rows, vcols[h]] = (y * (r * jax.nn.sigmoid(r))).astype(o_ref.dtype)
            else:
                o_ref[rows, vcols[h]] = o[h].astype(o_ref.dtype)


def gla_direction(proj, gates, wg2, bg, *, batch, tb, reverse, o_fwd=None, out_gain=None):
    T = proj.shape[0]
    S = T // batch
    nb = S // tb
    final = o_fwd is not None
    H, dk, dv = GLA_HEADS, GLA_HK, GLA_HV
    wk, wv = H * dk, H * dv

    def tok(b, i):
        return b * nb + ((nb - 1 - i) if reverse else i)

    in_specs = [
        pl.BlockSpec((tb, wk), lambda b, i: (tok(b, i), 0)),
        pl.BlockSpec((tb, wk), lambda b, i: (tok(b, i), 1)),
        pl.BlockSpec((tb, wv), lambda b, i: (tok(b, i), (2 * wk) // wv)),
        pl.BlockSpec((tb, gates.shape[1]), lambda b, i: (tok(b, i), 0)),
        pl.BlockSpec((GLA_RANK, wk), lambda b, i: (0, 0)),
        pl.BlockSpec((1, wk), lambda b, i: (0, 0)),
    ]
    args = [proj, proj, proj, gates, wg2, bg]
    if final:
        in_specs += [
            pl.BlockSpec((tb, wv), lambda b, i: (tok(b, i), (2 * wk) // wv + 1)),
            pl.BlockSpec((tb, wv), lambda b, i: (tok(b, i), 0)),
            pl.BlockSpec((1, wv), lambda b, i: (0, 0)),
        ]
        args += [proj, o_fwd, out_gain]
    kern = functools.partial(_gla_kernel, reverse=reverse, final=final,
                             gate_off=GLA_RANK if reverse else 0)
    return pl.pallas_call(
        kern,
        grid=(batch, nb),
        in_specs=in_specs,
        out_specs=pl.BlockSpec((tb, wv), lambda b, i: (tok(b, i), 0)),
        out_shape=jax.ShapeDtypeStruct((T, wv), BF16),
        scratch_shapes=[pltpu.VMEM((H, dv, dk), F32)],
        compiler_params=_cparams(("parallel", "arbitrary")),
        name="gla_bwd" if reverse else "gla_fwd",
    )(*args)


NA_QROWS = 8
NA_DR = 2 * NA_ROWS - 1
NA_DC = 2 * NA_COLS - 1


def _natten_bias_kernel(rpb_ref, o_ref):
    W = GRID_W
    base = (pl.program_id(0) * NA_HEADS + pl.program_id(1)) * (NA_DR * NA_DC)
    qc = lax.broadcasted_iota(jnp.int32, (W, 2 * W), 0)
    ln = lax.broadcasted_iota(jnp.int32, (W, 2 * W), 1)
    kc = ln & (W - 1)
    dc_idx = jnp.clip(kc - qc, -(NA_COLS - 1), NA_COLS - 1) + (NA_COLS - 1)
    cs = jnp.clip(qc - NA_COLS // 2, 0, W - NA_COLS)
    in_win = (kc >= cs) & (kc < cs + NA_COLS)
    left = ln < W
    toeplitz = []
    for dr in range(NA_DR):
        t = jnp.full((W, 2 * W), MASK_VALUE, F32)
        for dc in range(NA_DC):
            t = jnp.where(dc_idx == dc, rpb_ref[base + dr * NA_DC + dc], t)
        toeplitz.append(jnp.where(in_win, t, MASK_VALUE))
    for dr0 in range(NA_ROWS):
        for jp in range(NA_ROWS // 2):
            o_ref[dr0, :, jp * 2 * W:(jp + 1) * 2 * W] = jnp.where(
                left, toeplitz[dr0 + 2 * jp], toeplitz[dr0 + 2 * jp + 1])


def natten_bias_tables(rpb):
    L, H = rpb.shape[:2]
    W = GRID_W
    return pl.pallas_call(
        _natten_bias_kernel,
        grid=(L, H),
        in_specs=[pl.BlockSpec(memory_space=pltpu.SMEM)],
        out_specs=pl.BlockSpec((None, None, NA_ROWS, W, NA_ROWS * W), lambda l, h: (l, h, 0, 0, 0)),
        out_shape=jax.ShapeDtypeStruct((L, H, NA_ROWS, W, NA_ROWS * W), F32),
        compiler_params=_cparams(("parallel", "parallel")),
        name="natten_bias",
    )(rpb.reshape(-1))


def _natten_kernel(q_ref, kp_ref, kc_ref, kn_ref, vp_ref, vc_ref, vn_ref, wb_ref, go_ref, o_ref,
                   o_scr, *, rows):
    W = GRID_W
    nq = q_ref.shape[0]
    lw = 2 * NA_HD
    nwin = NA_ROWS * W

    def window(refs, start, cols):
        parts = []
        for b, ref in enumerate(refs):
            lo, hi = max(start, b * nq), min(start + nwin, (b + 1) * nq)
            if lo < hi:
                parts.append(ref[lo - b * nq:hi - b * nq, cols])
        return parts[0] if len(parts) == 1 else jnp.concatenate(parts, axis=0)

    k_refs = (kp_ref, kc_ref, kn_ref)
    v_refs = (vp_ref, vc_ref, vn_ref)
    head0 = lax.broadcasted_iota(jnp.int32, (1, lw), 1) < NA_HD
    blk = pl.program_id(1)
    nblk = rows // NA_QROWS

    def step(r0):
        for a in range(NA_QROWS):
            rs = min(max(r0 + a - NA_ROWS // 2, 0), rows - NA_ROWS)
            off = rs - (r0 - NA_QROWS)
            dr0 = rs - (r0 + a) + (NA_ROWS - 1)
            qrows = slice(a * W, (a + 1) * W)
            pairs = range(NA_HEADS // 2)
            cols = [slice(p * lw, (p + 1) * lw) for p in pairs]
            s = []
            for p in pairs:
                qa = q_ref[qrows, cols[p]]
                zero = jnp.zeros_like(qa)
                lhs = jnp.concatenate([jnp.where(head0, qa, zero), jnp.where(head0, zero, qa)], axis=0)
                sp = lax.dot_general(lhs, window(k_refs, off * W, cols[p]), NT_DIMS,
                                     preferred_element_type=F32)
                s.append(sp + jnp.concatenate([wb_ref[2 * p, dr0], wb_ref[2 * p + 1, dr0]], axis=0))
            m = [jnp.max(sp, axis=-1, keepdims=True) for sp in s]
            e = [jnp.exp(sp - mp) for sp, mp in zip(s, m)]
            l = [jnp.sum(ep, axis=-1, keepdims=True) for ep in e]
            r = [jnp.dot(e[p].astype(BF16), window(v_refs, off * W, cols[p]), preferred_element_type=F32)
                 for p in pairs]
            for p in pairs:
                rp = r[p] / l[p]
                o_scr[qrows, cols[p]] = jnp.where(head0, rp[0:W], rp[W:2 * W])

    pl.when(blk == 0)(lambda: step(0))
    pl.when(jnp.logical_and(blk > 0, blk < nblk - 1))(lambda: step(NA_QROWS))
    pl.when(blk == nblk - 1)(lambda: step(rows - NA_QROWS))

    gw = V7X_MXU_WIDTH
    ones_bd = _head_pair_ones(gw)
    for g in range(o_ref.shape[1] // gw):
        cols = slice(g * gw, (g + 1) * gw)
        o_ref[:, cols] = _head_pair_rms(o_scr[:, cols], go_ref[:, cols], ones_bd).astype(o_ref.dtype)


def natten(proj, wb, go, *, layer, batch, col0):
    T = proj.shape[0]
    S = T // batch
    rows = S // GRID_W
    nq = NA_QROWS * GRID_W
    nblk = S // nq
    width = NA_HEADS * NA_HD
    cq = col0 // width

    def prev(i):
        return jnp.maximum(i - 1, 0)

    def nxt(i):
        return jnp.minimum(i + 1, nblk - 1)

    in_specs = [
        pl.BlockSpec((nq, width), lambda b, i: (b * nblk + i, cq)),
        pl.BlockSpec((nq, width), lambda b, i: (b * nblk + prev(i), cq + 1)),
        pl.BlockSpec((nq, width), lambda b, i: (b * nblk + i, cq + 1)),
        pl.BlockSpec((nq, width), lambda b, i: (b * nblk + nxt(i), cq + 1)),
        pl.BlockSpec((nq, width), lambda b, i: (b * nblk + prev(i), cq + 2)),
        pl.BlockSpec((nq, width), lambda b, i: (b * nblk + i, cq + 2)),
        pl.BlockSpec((nq, width), lambda b, i: (b * nblk + nxt(i), cq + 2)),
        pl.BlockSpec((None,) + wb.shape[1:], lambda b, i: (layer, 0, 0, 0, 0)),
        pl.BlockSpec((1, width), lambda b, i: (0, 0)),
    ]
    return pl.pallas_call(
        functools.partial(_natten_kernel, rows=rows),
        grid=(batch, nblk),
        in_specs=in_specs,
        out_specs=pl.BlockSpec((nq, width), lambda b, i: (b * nblk + i, 0)),
        out_shape=jax.ShapeDtypeStruct((T, width), BF16),
        scratch_shapes=[pltpu.VMEM((nq, width), F32)],
        compiler_params=_cparams(("parallel", "arbitrary")),
        name="natten",
    )(proj, proj, proj, proj, proj, proj, proj, wb, go)


def _mem_kv_kernel(mem_ref, g_ref, w_ref, gk_ref, k_ref, v_ref):
    mn = _rms(mem_ref[...], g_ref[...]).astype(BF16)
    kv = jnp.dot(mn, w_ref[...], preferred_element_type=F32)
    width = k_ref.shape[1]
    for h in range(MEM_HEADS):
        cols = slice(h * MEM_HD, (h + 1) * MEM_HD)
        k_ref[:, cols] = _rms(kv[:, cols], gk_ref[...]).astype(BF16)
    v_ref[...] = kv[:, width:].astype(BF16)


def mem_kv(mem2d, gain, wkv, gk):
    n, D = mem2d.shape
    L = wkv.shape[0]
    width = wkv.shape[2] // 2
    return pl.pallas_call(
        _mem_kv_kernel,
        grid=(L,),
        in_specs=[
            pl.BlockSpec((n, D), lambda l: (0, 0)),
            pl.BlockSpec((None, 1, D), lambda l: (l, 0, 0)),
            pl.BlockSpec((None, D, 2 * width), lambda l: (l, 0, 0)),
            pl.BlockSpec((None, 1, MEM_HD), lambda l: (l, 0, 0)),
        ],
        out_specs=[pl.BlockSpec((None, n, width), lambda l: (l, 0, 0)),
                   pl.BlockSpec((None, n, width), lambda l: (l, 0, 0))],
        out_shape=[jax.ShapeDtypeStruct((L, n, width), BF16), jax.ShapeDtypeStruct((L, n, width), BF16)],
        compiler_params=_cparams(("parallel",)),
        name="mem_kv",
    )(mem2d, gain, wkv, gk)


def _out_proj_kernel(x_ref, ya_ref, yb_ref, yc_ref, w_ref, g_ref, o_ref, h_ref, *, sub):
    tm = x_ref.shape[0]
    wa, wb = ya_ref.shape[1], yb_ref.shape[1]
    for s in range(tm // sub):
        rows = slice(s * sub, (s + 1) * sub)
        acc = x_ref[rows, :] + jnp.dot(ya_ref[rows, :], w_ref[0:wa, :], preferred_element_type=F32)
        acc += jnp.dot(yb_ref[rows, :], w_ref[wa:wa + wb, :], preferred_element_type=F32)
        acc += jnp.dot(yc_ref[rows, :], w_ref[wa + wb:, :], preferred_element_type=F32)
        o_ref[rows, :] = acc
        h_ref[rows, :] = _rms(acc, g_ref[...]).astype(BF16)


def out_proj(x, y_gla, y_na, y_mem, w_out, gain, *, layer, tm, sub):
    T, D = x.shape
    wa, wb, wc = y_gla.shape[1], y_na.shape[1], y_mem.shape[1]
    return pl.pallas_call(
        functools.partial(_out_proj_kernel, sub=sub),
        grid=(T // tm,),
        in_specs=[
            pl.BlockSpec((tm, D), lambda i: (i, 0)),
            pl.BlockSpec((tm, wa), lambda i: (i, 0)),
            pl.BlockSpec((tm, wb), lambda i: (i, 0)),
            pl.BlockSpec((tm, wc), lambda i: (i, 0)),
            pl.BlockSpec((None, wa + wb + wc, D), lambda i: (layer, 0, 0), pipeline_mode=pl.Buffered(1)),
            pl.BlockSpec((1, D), lambda i: (0, 0)),
        ],
        out_specs=[pl.BlockSpec((tm, D), lambda i: (i, 0)), pl.BlockSpec((tm, D), lambda i: (i, 0))],
        out_shape=[jax.ShapeDtypeStruct((T, D), F32), jax.ShapeDtypeStruct((T, D), BF16)],
        compiler_params=_cparams(("parallel",)),
        name="out_proj",
    )(x, y_gla, y_na, y_mem, w_out, gain)


def _ffn_kernel(x_ref, h_ref, w1_ref, w3_ref, w2_ref, o_ref):
    @pl.when(pl.program_id(1) == 0)
    def _():
        o_ref[...] = x_ref[...]

    h = h_ref[...]
    gate = jnp.dot(h, w1_ref[...], preferred_element_type=F32)
    up = jnp.dot(h, w3_ref[...], preferred_element_type=F32)
    act = (gate * jax.nn.sigmoid(gate) * up).astype(BF16)
    o_ref[...] += jnp.dot(act, w2_ref[...], preferred_element_type=F32)


def ffn(x, h, w13, w2, *, layer, tm, tf):
    T, D = x.shape
    F = w2.shape[1]
    nf = F // tf
    return pl.pallas_call(
        _ffn_kernel,
        grid=(T // tm, nf),
        in_specs=[
            pl.BlockSpec((tm, D), lambda i, f: (i, 0)),
            pl.BlockSpec((tm, D), lambda i, f: (i, 0)),
            pl.BlockSpec((None, D, tf), lambda i, f: (layer, 0, f)),
            pl.BlockSpec((None, D, tf), lambda i, f: (layer, 0, nf + f)),
            pl.BlockSpec((None, tf, D), lambda i, f: (layer, f, 0)),
        ],
        out_specs=pl.BlockSpec((tm, D), lambda i, f: (i, 0)),
        out_shape=jax.ShapeDtypeStruct((T, D), F32),
        compiler_params=_cparams(("parallel", "arbitrary")),
        name="ffn",
    )(x, h, w13, w13, w2)


IN_PROJ_TN = 1024


def kernel(x, mem, attn_norm, w_in, gla_wg2_f, gla_bg_f, gla_wg2_b, gla_bg_b, gla_out_norm,
           na_q_norm, na_k_norm, na_rpb, na_out_norm, mem_norm, mem_wkv, mem_q_norm, mem_k_norm,
           mem_out_norm, w_out, ffn_norm, ffn_w13, ffn_w2):
    B, S, D = x.shape
    depth = w_in.shape[0]
    T = B * S
    gla_dk = GLA_HEADS * GLA_HK
    gla_dv = GLA_HEADS * GLA_HV
    na_w = NA_HEADS * NA_HD
    mem_w = MEM_HEADS * MEM_HD
    gate0 = 2 * gla_dk + 2 * gla_dv
    gate1 = gate0 + 2 * GLA_RANK
    na_col0 = gate0
    mem_col0 = gate0 + 3 * na_w
    assert w_in.shape[2] - 2 * GLA_RANK == mem_col0 + mem_w and gla_dv + na_w + mem_w == w_out.shape[1]
    assert na_col0 % IN_PROJ_TN == 0 and 2 * na_w == IN_PROJ_TN

    w_in_t = jnp.swapaxes(w_in, 1, 2).astype(BF16)
    w_out_b = w_out.astype(BF16)
    w13_b = ffn_w13.astype(BF16)
    w2_b = ffn_w2.astype(BF16)
    wkv_b = mem_wkv.astype(BF16)
    gqk = jnp.concatenate([jnp.tile(na_q_norm * (NA_HD ** -0.5), (1, NA_HEADS)),
                           jnp.tile(na_k_norm, (1, NA_HEADS))], axis=1)
    wb = natten_bias_tables(na_rpb)

    xf = x.reshape(T, D)
    k_m, v_m = mem_kv(mem.reshape(B * mem.shape[1], D), mem_norm[:, None], wkv_b, mem_k_norm[:, None])
    for l in range(depth):
        proj, gates, y_mem = in_proj(xf, attn_norm[l][None], w_in_t, gqk[l][None], k_m, v_m,
                                     mem_q_norm[l][None], mem_out_norm[l][None], layer=l, batch=B,
                                     tm=512, sub=256, tn=IN_PROJ_TN, qk_tile=na_col0 // IN_PROJ_TN,
                                     gate0=gate0, ngate=gate1 - gate0)
        o_f = gla_direction(proj, gates, gla_wg2_f[l], gla_bg_f[l][None], batch=B, tb=1024, reverse=False)
        y_gla = gla_direction(proj, gates, gla_wg2_b[l], gla_bg_b[l][None], batch=B, tb=1024, reverse=True,
                              o_fwd=o_f, out_gain=gla_out_norm[l][None])
        y_na = natten(proj, wb, na_out_norm[l][None], layer=l, batch=B, col0=na_col0)
        xf, hf = out_proj(xf, y_gla, y_na, y_mem, w_out_b, ffn_norm[l][None], layer=l, tm=512, sub=256)
        xf = ffn(xf, hf, w13_b, w2_b, layer=l, tm=512, tf=512)
    return xf.reshape(B, S, D)
```

```python
import functools

import jax
import jax.numpy as jnp
from jax import lax
from jax.experimental import pallas as pl
from jax.experimental.pallas import tpu as pltpu

F32 = jnp.float32
BF16 = jnp.bfloat16

RMS_EPS = 1e-6
MASK_VALUE = -1e30

GRID_W = 64
GLA_HEADS = 4
GLA_HK = 128
GLA_HV = 256
GLA_RANK = 16
GLA_TAU = 16.0
GLA_CHUNK = 64
NA_HD = 64
NA_HEADS = 8
NA_ROWS = 8
NA_COLS = 16
MEM_HEADS = 4
MEM_HD = 128

V7X_MXU_WIDTH = 256
V7X_VMEM_BYTES = 64 * 1024 * 1024
VMEM_LIMIT_BYTES = V7X_VMEM_BYTES - 8 * 1024 * 1024

NT_DIMS = (((1,), (1,)), ((), ()))


def _cparams(sem):
    return pltpu.CompilerParams(dimension_semantics=sem, vmem_limit_bytes=VMEM_LIMIT_BYTES)


def _rms(x, gain):
    ms = jnp.mean(x * x, axis=-1, keepdims=True)
    return (x * lax.rsqrt(ms + RMS_EPS)) * gain


def _head_pair_ones(width=2 * NA_HD):
    r = lax.broadcasted_iota(jnp.int32, (width, width), 0) // NA_HD
    c = lax.broadcasted_iota(jnp.int32, (width, width), 1) // NA_HD
    return jnp.where(r == c, 1.0, 0.0).astype(BF16)


def _head_pair_rms(x, gain, ones_bd):
    ms = jnp.dot((x * x).astype(BF16), ones_bd, preferred_element_type=F32) * (1.0 / NA_HD)
    return (x * lax.rsqrt(ms + RMS_EPS)) * gain


CAST_ROW_UNIT = 16


def _cast_streams(cast_next, steps, step_of):
    in_specs, out_specs, out_shapes, operands = [], [], [], []
    for w, layer in cast_next:
        _, rows, cols = w.shape
        units = rows // CAST_ROW_UNIT
        assert rows % CAST_ROW_UNIT == 0
        mult = min(d for d in range(1, units + 1) if units % d == 0 and units // d <= steps)
        nblk = units // mult
        per = steps // nblk

        def blk(*g, per=per, nblk=nblk):
            return jnp.minimum(step_of(*g) // per, nblk - 1)

        in_specs.append(pl.BlockSpec((None, mult * CAST_ROW_UNIT, cols),
                                     lambda *g, blk=blk, layer=layer: (layer, blk(*g), 0)))
        out_specs.append(pl.BlockSpec((mult * CAST_ROW_UNIT, cols), lambda *g, blk=blk: (blk(*g), 0)))
        out_shapes.append(jax.ShapeDtypeStruct((rows, cols), BF16))
        operands.append(w)
    return in_specs, out_specs, out_shapes, operands


def _with_cast_streams(body, n_in, n_out, n_cast):
    def kern(*refs):
        ins, rest = refs[:n_in], refs[n_in:]
        cast_in, rest = rest[:n_cast], rest[n_cast:]
        outs, rest = rest[:n_out], rest[n_out:]
        cast_out, scratch = rest[:n_cast], rest[n_cast:]
        body(*ins, *outs, *scratch)
        for src, dst in zip(cast_in, cast_out):
            dst[...] = src[...].astype(BF16)
    return kern


def _mem_attention(mq, km_ref, vm_ref, gq_ref, go_ref, y_ref, rows):
    heads = range(MEM_HEADS)
    cols = [slice(h * MEM_HD, (h + 1) * MEM_HD) for h in heads]
    q = [(_rms(mq[:, c], gq_ref[...]) * (MEM_HD ** -0.5)).astype(BF16) for c in cols]
    s = [lax.dot_general(q[h], km_ref[:, cols[h]], NT_DIMS, preferred_element_type=F32) for h in heads]
    p = [jnp.exp(sh - jnp.max(sh, axis=-1, keepdims=True)) for sh in s]
    l = [jnp.sum(ph, axis=-1, keepdims=True) for ph in p]
    o = [jnp.dot(p[h].astype(BF16), vm_ref[:, cols[h]], preferred_element_type=F32) / l[h] for h in heads]
    for h in heads:
        y_ref[rows, cols[h]] = _rms(o[h], go_ref[:, cols[h]]).astype(y_ref.dtype)


def _in_proj_kernel(x_ref, g_ref, wt_ref, gqk_ref, km_ref, vm_ref, gmq_ref, gmo_ref,
                    proj_ref, gate_ref, ymem_ref, *, sub, tn, qk_tile, gate0):
    tm = x_ref.shape[0]
    ngate = gate_ref.shape[1]
    lw = V7X_MXU_WIDTH
    ones_bd = _head_pair_ones(lw)
    n_proj = proj_ref.shape[1]
    ntile = (wt_ref.shape[0] - ngate) // tn
    for s in range(tm // sub):
        rows = slice(s * sub, (s + 1) * sub)
        xn = _rms(x_ref[rows, :], g_ref[...]).astype(BF16)
        gate_ref[rows, :] = lax.dot_general(xn, wt_ref[gate0:gate0 + ngate, :], NT_DIMS,
                                            preferred_element_type=F32)
        for n in [ntile - 1] + list(range(ntile - 1)):
            w0 = n * tn + (ngate if n * tn >= gate0 else 0)
            acc = lax.dot_general(xn, wt_ref[w0:w0 + tn, :], NT_DIMS, preferred_element_type=F32)
            keep = min(tn, n_proj - n * tn)
            if n == qk_tile:
                for c in range(tn // lw):
                    cols = slice(c * lw, (c + 1) * lw)
                    proj_ref[rows, n * tn + c * lw:n * tn + (c + 1) * lw] = _head_pair_rms(
                        acc[:, cols], gqk_ref[:, cols], ones_bd).astype(BF16)
            else:
                proj_ref[rows, n * tn:n * tn + keep] = acc[:, :keep].astype(BF16)
            if keep < tn:
                _mem_attention(acc[:, keep:], km_ref, vm_ref, gmq_ref, gmo_ref, ymem_ref, rows)


def in_proj(x, gain, w_t, gqk, k_m, v_m, gmq, gmo, *, layer, batch, tm, sub, tn, qk_tile, gate0, ngate):
    T, D = x.shape
    n_in = w_t.shape[0]
    N = n_in - ngate
    G = ngate
    assert gate0 % tn == 0 and N % tn == 0
    mem_w = MEM_HEADS * MEM_HD
    n_mem = k_m.shape[1] // batch
    per_b = (T // batch) // tm
    return pl.pallas_call(
        functools.partial(_in_proj_kernel, sub=sub, tn=tn, qk_tile=qk_tile, gate0=gate0),
        grid=(T // tm,),
        in_specs=[
            pl.BlockSpec((tm, D), lambda i: (i, 0)),
            pl.BlockSpec((1, D), lambda i: (0, 0)),
            pl.BlockSpec((n_in, D), lambda i: (0, 0), pipeline_mode=pl.Buffered(1)),
            pl.BlockSpec((1, tn), lambda i: (0, 0)),
            pl.BlockSpec((None, n_mem, mem_w), lambda i: (layer, i // per_b, 0)),
            pl.BlockSpec((None, n_mem, mem_w), lambda i: (layer, i // per_b, 0)),
            pl.BlockSpec((1, MEM_HD), lambda i: (0, 0)),
            pl.BlockSpec((1, mem_w), lambda i: (0, 0)),
        ],
        out_specs=[
            pl.BlockSpec((tm, N - mem_w), lambda i: (i, 0)),
            pl.BlockSpec((tm, G), lambda i: (i, 0)),
            pl.BlockSpec((tm, mem_w), lambda i: (i, 0)),
        ],
        out_shape=[jax.ShapeDtypeStruct((T, N - mem_w), BF16), jax.ShapeDtypeStruct((T, G), F32),
                   jax.ShapeDtypeStruct((T, mem_w), BF16)],
        compiler_params=_cparams(("parallel",)),
        name="in_proj",
    )(x, gain, w_t, gqk, k_m, v_m, gmq, gmo)


def _log_sigmoid(z):
    return jnp.minimum(z, 0.0) - jnp.log(1.0 + jnp.exp(-jnp.abs(z)))


def _gla_kernel(q_ref, k_ref, v_ref, gate_ref, wg_ref, bg_ref, *rest, reverse, final, gate_off):
    if final:
        r_ref, of_ref, gn_ref, o_ref, st_ref = rest
    else:
        o_ref, st_ref = rest
    C = GLA_CHUNK
    H, dk, dv = GLA_HEADS, GLA_HK, GLA_HV
    tb = q_ref.shape[0]
    nchunk = tb // C

    @pl.when(pl.program_id(1) == 0)
    def _():
        st_ref[...] = jnp.zeros_like(st_ref)

    g = gate_ref[:, gate_off:gate_off + GLA_RANK]
    z = jnp.dot(g.astype(BF16), wg_ref[...].astype(BF16), preferred_element_type=F32) + bg_ref[...]
    la = _log_sigmoid(z) * (1.0 / GLA_TAU)

    la_t = jnp.concatenate([la[c * C:(c + 1) * C, :] for c in range(nchunk)], axis=1)
    row = lax.broadcasted_iota(jnp.int32, (C, C), 0)
    col = lax.broadcasted_iota(jnp.int32, (C, C), 1)
    if reverse:
        tri = (col >= row)
        keep = col > row
    else:
        tri = (col <= row)
        keep = col <= row
    tri = jnp.where(tri, 1.0, 0.0).astype(BF16)
    la_hi = la_t.astype(BF16)
    la_lo = (la_t - la_hi.astype(F32)).astype(BF16)
    cum = (jnp.dot(tri, la_hi, preferred_element_type=F32)
           + jnp.dot(tri, la_lo, preferred_element_type=F32))

    order = range(nchunk - 1, -1, -1) if reverse else range(nchunk)
    heads = range(H)
    kcols = [slice(h * dk, (h + 1) * dk) for h in heads]
    vcols = [slice(h * dv, (h + 1) * dv) for h in heads]
    for c in order:
        rows = slice(c * C, (c + 1) * C)
        cum_c = cum[:, c * H * dk:(c + 1) * H * dk]
        last = cum_c[0:1, :] if reverse else cum_c[C - 1:C, :]
        qc = q_ref[rows, :].astype(F32) * (GLA_HK ** -0.5)
        kc = k_ref[rows, :].astype(F32)
        q_e = (qc * jnp.exp(cum_c)).astype(BF16)
        k_e = (kc * jnp.exp(-cum_c)).astype(BF16)
        k_end = (kc * jnp.exp(last - cum_c)).astype(BF16)
        decay = jnp.exp(last)
        vc = [v_ref[rows, vcols[h]] for h in heads]
        sc = [lax.dot_general(q_e[:, kcols[h]], k_e[:, kcols[h]], NT_DIMS, preferred_element_type=F32)
              for h in heads]
        sc = [jnp.where(keep, s, 0.0).astype(BF16) for s in sc]
        st = [st_ref[h] for h in heads]
        o = [jnp.dot(sc[h], vc[h], preferred_element_type=F32)
             + lax.dot_general(q_e[:, kcols[h]], st[h].astype(BF16), NT_DIMS, preferred_element_type=F32)
             for h in heads]
        for h in heads:
            st_ref[h] = (st[h] * decay[:, kcols[h]]
                         + jnp.dot(vc[h].T, k_end[:, kcols[h]], preferred_element_type=F32))
        for h in heads:
            if final:
                oh = o[h] + of_ref[rows, vcols[h]].astype(F32)
                y = _rms(oh, gn_ref[:, vcols[h]])
                r = r_ref[rows, vcols[h]].astype(F32)
                o_ref[rows, vcols[h]] = (y * (r * jax.nn.sigmoid(r))).astype(o_ref.dtype)
            else:
                o_ref[rows, vcols[h]] = o[h].astype(o_ref.dtype)


def gla_direction(proj, gates, wg2, bg, *, batch, tb, reverse, o_fwd=None, out_gain=None, cast_next=()):
    T = proj.shape[0]
    S = T // batch
    nb = S // tb
    final = o_fwd is not None
    H, dk, dv = GLA_HEADS, GLA_HK, GLA_HV
    wk, wv = H * dk, H * dv

    def tok(b, i):
        return b * nb + ((nb - 1 - i) if reverse else i)

    in_specs = [
        pl.BlockSpec((tb, wk), lambda b, i: (tok(b, i), 0)),
        pl.BlockSpec((tb, wk), lambda b, i: (tok(b, i), 1)),
        pl.BlockSpec((tb, wv), lambda b, i: (tok(b, i), (2 * wk) // wv)),
        pl.BlockSpec((tb, gates.shape[1]), lambda b, i: (tok(b, i), 0)),
        pl.BlockSpec((GLA_RANK, wk), lambda b, i: (0, 0)),
        pl.BlockSpec((1, wk), lambda b, i: (0, 0)),
    ]
    args = [proj, proj, proj, gates, wg2, bg]
    if final:
        in_specs += [
            pl.BlockSpec((tb, wv), lambda b, i: (tok(b, i), (2 * wk) // wv + 1)),
            pl.BlockSpec((tb, wv), lambda b, i: (tok(b, i), 0)),
            pl.BlockSpec((1, wv), lambda b, i: (0, 0)),
        ]
        args += [proj, o_fwd, out_gain]
    kern = functools.partial(_gla_kernel, reverse=reverse, final=final,
                             gate_off=GLA_RANK if reverse else 0)
    c_in, c_out, c_shapes, c_args = _cast_streams(cast_next, batch * nb, lambda b, i: b * nb + i)
    outs = pl.pallas_call(
        _with_cast_streams(kern, len(args), 1, len(c_args)),
        grid=(batch, nb),
        in_specs=in_specs + c_in,
        out_specs=[pl.BlockSpec((tb, wv), lambda b, i: (tok(b, i), 0))] + c_out,
        out_shape=[jax.ShapeDtypeStruct((T, wv), BF16)] + c_shapes,
        scratch_shapes=[pltpu.VMEM((H, dv, dk), F32)],
        compiler_params=_cparams(("arbitrary", "arbitrary")),
        name="gla_bwd" if reverse else "gla_fwd",
    )(*args, *c_args)
    return outs[0], outs[1:]


NA_QROWS = 8
NA_DR = 2 * NA_ROWS - 1
NA_DC = 2 * NA_COLS - 1


def _natten_bias_kernel(rpb_ref, o_ref):
    W = GRID_W
    base = (pl.program_id(0) * NA_HEADS + pl.program_id(1)) * (NA_DR * NA_DC)
    qc = lax.broadcasted_iota(jnp.int32, (W, 2 * W), 0)
    ln = lax.broadcasted_iota(jnp.int32, (W, 2 * W), 1)
    kc = ln & (W - 1)
    dc_idx = jnp.clip(kc - qc, -(NA_COLS - 1), NA_COLS - 1) + (NA_COLS - 1)
    cs = jnp.clip(qc - NA_COLS // 2, 0, W - NA_COLS)
    in_win = (kc >= cs) & (kc < cs + NA_COLS)
    left = ln < W
    toeplitz = []
    for dr in range(NA_DR):
        t = jnp.full((W, 2 * W), MASK_VALUE, F32)
        for dc in range(NA_DC):
            t = jnp.where(dc_idx == dc, rpb_ref[base + dr * NA_DC + dc], t)
        toeplitz.append(jnp.where(in_win, t, MASK_VALUE))
    for dr0 in range(NA_ROWS):
        for jp in range(NA_ROWS // 2):
            o_ref[dr0, :, jp * 2 * W:(jp + 1) * 2 * W] = jnp.where(
                left, toeplitz[dr0 + 2 * jp], toeplitz[dr0 + 2 * jp + 1])


def natten_bias_tables(rpb):
    L, H = rpb.shape[:2]
    W = GRID_W
    return pl.pallas_call(
        _natten_bias_kernel,
        grid=(L, H),
        in_specs=[pl.BlockSpec(memory_space=pltpu.SMEM)],
        out_specs=pl.BlockSpec((None, None, NA_ROWS, W, NA_ROWS * W), lambda l, h: (l, h, 0, 0, 0)),
        out_shape=jax.ShapeDtypeStruct((L, H, NA_ROWS, W, NA_ROWS * W), F32),
        compiler_params=_cparams(("parallel", "parallel")),
        name="natten_bias",
    )(rpb.reshape(-1))


def _natten_kernel(q_ref, kp_ref, kc_ref, kn_ref, vp_ref, vc_ref, vn_ref, wb_ref, go_ref, o_ref,
                   o_scr, *, rows):
    W = GRID_W
    nq = q_ref.shape[0]
    lw = 2 * NA_HD
    nwin = NA_ROWS * W

    def window(refs, start, cols):
        parts = []
        for b, ref in enumerate(refs):
            lo, hi = max(start, b * nq), min(start + nwin, (b + 1) * nq)
            if lo < hi:
                parts.append(ref[lo - b * nq:hi - b * nq, cols])
        return parts[0] if len(parts) == 1 else jnp.concatenate(parts, axis=0)

    k_refs = (kp_ref, kc_ref, kn_ref)
    v_refs = (vp_ref, vc_ref, vn_ref)
    head0 = lax.broadcasted_iota(jnp.int32, (1, lw), 1) < NA_HD
    blk = pl.program_id(1)
    nblk = rows // NA_QROWS

    def step(r0):
        for a in range(NA_QROWS):
            rs = min(max(r0 + a - NA_ROWS // 2, 0), rows - NA_ROWS)
            off = rs - (r0 - NA_QROWS)
            dr0 = rs - (r0 + a) + (NA_ROWS - 1)
            qrows = slice(a * W, (a + 1) * W)
            pairs = range(NA_HEADS // 2)
            cols = [slice(p * lw, (p + 1) * lw) for p in pairs]
            s = []
            for p in pairs:
                qa = q_ref[qrows, cols[p]]
                zero = jnp.zeros_like(qa)
                lhs = jnp.concatenate([jnp.where(head0, qa, zero), jnp.where(head0, zero, qa)], axis=0)
                sp = lax.dot_general(lhs, window(k_refs, off * W, cols[p]), NT_DIMS,
                                     preferred_element_type=F32)
                s.append(sp + jnp.concatenate([wb_ref[2 * p, dr0], wb_ref[2 * p + 1, dr0]], axis=0))
            m = [jnp.max(sp, axis=-1, keepdims=True) for sp in s]
            e = [jnp.exp(sp - mp) for sp, mp in zip(s, m)]
            l = [jnp.sum(ep, axis=-1, keepdims=True) for ep in e]
            r = [jnp.dot(e[p].astype(BF16), window(v_refs, off * W, cols[p]), preferred_element_type=F32)
                 for p in pairs]
            for p in pairs:
                rp = r[p] / l[p]
                o_scr[qrows, cols[p]] = jnp.where(head0, rp[0:W], rp[W:2 * W])

    pl.when(blk == 0)(lambda: step(0))
    pl.when(jnp.logical_and(blk > 0, blk < nblk - 1))(lambda: step(NA_QROWS))
    pl.when(blk == nblk - 1)(lambda: step(rows - NA_QROWS))

    gw = V7X_MXU_WIDTH
    ones_bd = _head_pair_ones(gw)
    for g in range(o_ref.shape[1] // gw):
        cols = slice(g * gw, (g + 1) * gw)
        o_ref[:, cols] = _head_pair_rms(o_scr[:, cols], go_ref[:, cols], ones_bd).astype(o_ref.dtype)


def natten(proj, wb, go, *, layer, batch, col0, cast_next=()):
    T = proj.shape[0]
    S = T // batch
    rows = S // GRID_W
    nq = NA_QROWS * GRID_W
    nblk = S // nq
    width = NA_HEADS * NA_HD
    cq = col0 // width

    def prev(i):
        return jnp.maximum(i - 1, 0)

    def nxt(i):
        return jnp.minimum(i + 1, nblk - 1)

    in_specs = [
        pl.BlockSpec((nq, width), lambda b, i: (b * nblk + i, cq)),
        pl.BlockSpec((nq, width), lambda b, i: (b * nblk + prev(i), cq + 1)),
        pl.BlockSpec((nq, width), lambda b, i: (b * nblk + i, cq + 1)),
        pl.BlockSpec((nq, width), lambda b, i: (b * nblk + nxt(i), cq + 1)),
        pl.BlockSpec((nq, width), lambda b, i: (b * nblk + prev(i), cq + 2)),
        pl.BlockSpec((nq, width), lambda b, i: (b * nblk + i, cq + 2)),
        pl.BlockSpec((nq, width), lambda b, i: (b * nblk + nxt(i), cq + 2)),
        pl.BlockSpec((None,) + wb.shape[1:], lambda b, i: (layer, 0, 0, 0, 0)),
        pl.BlockSpec((1, width), lambda b, i: (0, 0)),
    ]
    args = [proj, proj, proj, proj, proj, proj, proj, wb, go]
    c_in, c_out, c_shapes, c_args = _cast_streams(cast_next, batch * nblk, lambda b, i: b * nblk + i)
    outs = pl.pallas_call(
        _with_cast_streams(functools.partial(_natten_kernel, rows=rows), len(args), 1, len(c_args)),
        grid=(batch, nblk),
        in_specs=in_specs + c_in,
        out_specs=[pl.BlockSpec((nq, width), lambda b, i: (b * nblk + i, 0))] + c_out,
        out_shape=[jax.ShapeDtypeStruct((T, width), BF16)] + c_shapes,
        scratch_shapes=[pltpu.VMEM((nq, width), F32)],
        compiler_params=_cparams(("arbitrary", "arbitrary")),
        name="natten",
    )(*args, *c_args)
    return outs[0], outs[1:]


def _mem_kv_kernel(mem_ref, g_ref, w_ref, gk_ref, k_ref, v_ref):
    mn = _rms(mem_ref[...], g_ref[...]).astype(BF16)
    kv = jnp.dot(mn, w_ref[...], preferred_element_type=F32)
    width = k_ref.shape[1]
    for h in range(MEM_HEADS):
        cols = slice(h * MEM_HD, (h + 1) * MEM_HD)
        k_ref[:, cols] = _rms(kv[:, cols], gk_ref[...]).astype(BF16)
    v_ref[...] = kv[:, width:].astype(BF16)


def mem_kv(mem2d, gain, wkv, gk):
    n, D = mem2d.shape
    L = wkv.shape[0]
    width = wkv.shape[2] // 2
    return pl.pallas_call(
        _mem_kv_kernel,
        grid=(L,),
        in_specs=[
            pl.BlockSpec((n, D), lambda l: (0, 0)),
            pl.BlockSpec((None, 1, D), lambda l: (l, 0, 0)),
            pl.BlockSpec((None, D, 2 * width), lambda l: (l, 0, 0)),
            pl.BlockSpec((None, 1, MEM_HD), lambda l: (l, 0, 0)),
        ],
        out_specs=[pl.BlockSpec((None, n, width), lambda l: (l, 0, 0)),
                   pl.BlockSpec((None, n, width), lambda l: (l, 0, 0))],
        out_shape=[jax.ShapeDtypeStruct((L, n, width), BF16), jax.ShapeDtypeStruct((L, n, width), BF16)],
        compiler_params=_cparams(("parallel",)),
        name="mem_kv",
    )(mem2d, gain, wkv, gk)


def _out_proj_kernel(x_ref, ya_ref, yb_ref, yc_ref, w_ref, g_ref, o_ref, h_ref, *, sub):
    tm = x_ref.shape[0]
    wa, wb = ya_ref.shape[1], yb_ref.shape[1]
    for s in range(tm // sub):
        rows = slice(s * sub, (s + 1) * sub)
        acc = x_ref[rows, :] + jnp.dot(ya_ref[rows, :], w_ref[0:wa, :], preferred_element_type=F32)
        acc += jnp.dot(yb_ref[rows, :], w_ref[wa:wa + wb, :], preferred_element_type=F32)
        acc += jnp.dot(yc_ref[rows, :], w_ref[wa + wb:, :], preferred_element_type=F32)
        o_ref[rows, :] = acc
        h_ref[rows, :] = _rms(acc, g_ref[...]).astype(BF16)


def out_proj(x, y_gla, y_na, y_mem, w_out, gain, *, tm, sub):
    T, D = x.shape
    wa, wb, wc = y_gla.shape[1], y_na.shape[1], y_mem.shape[1]
    return pl.pallas_call(
        functools.partial(_out_proj_kernel, sub=sub),
        grid=(T // tm,),
        in_specs=[
            pl.BlockSpec((tm, D), lambda i: (i, 0)),
            pl.BlockSpec((tm, wa), lambda i: (i, 0)),
            pl.BlockSpec((tm, wb), lambda i: (i, 0)),
            pl.BlockSpec((tm, wc), lambda i: (i, 0)),
            pl.BlockSpec((wa + wb + wc, D), lambda i: (0, 0), pipeline_mode=pl.Buffered(1)),
            pl.BlockSpec((1, D), lambda i: (0, 0)),
        ],
        out_specs=[pl.BlockSpec((tm, D), lambda i: (i, 0)), pl.BlockSpec((tm, D), lambda i: (i, 0))],
        out_shape=[jax.ShapeDtypeStruct((T, D), F32), jax.ShapeDtypeStruct((T, D), BF16)],
        compiler_params=_cparams(("parallel",)),
        name="out_proj",
    )(x, y_gla, y_na, y_mem, w_out, gain)


def _ffn_kernel(x_ref, h_ref, w1_ref, w3_ref, w2_ref, o_ref):
    @pl.when(pl.program_id(1) == 0)
    def _():
        o_ref[...] = x_ref[...]

    h = h_ref[...]
    gate = jnp.dot(h, w1_ref[...], preferred_element_type=F32)
    up = jnp.dot(h, w3_ref[...], preferred_element_type=F32)
    act = (gate * jax.nn.sigmoid(gate) * up).astype(BF16)
    o_ref[...] += jnp.dot(act, w2_ref[...], preferred_element_type=F32)


def ffn(x, h, w13, w2, *, tm, tf):
    T, D = x.shape
    F = w2.shape[0]
    nf = F // tf
    return pl.pallas_call(
        _ffn_kernel,
        grid=(T // tm, nf),
        in_specs=[
            pl.BlockSpec((tm, D), lambda i, f: (i, 0)),
            pl.BlockSpec((tm, D), lambda i, f: (i, 0)),
            pl.BlockSpec((D, tf), lambda i, f: (0, f)),
            pl.BlockSpec((D, tf), lambda i, f: (0, nf + f)),
            pl.BlockSpec((tf, D), lambda i, f: (f, 0)),
        ],
        out_specs=pl.BlockSpec((tm, D), lambda i, f: (i, 0)),
        out_shape=jax.ShapeDtypeStruct((T, D), F32),
        compiler_params=_cparams(("parallel", "arbitrary")),
        name="ffn",
    )(x, h, w13, w13, w2)


IN_PROJ_TN = 1024


def kernel(x, mem, attn_norm, w_in, gla_wg2_f, gla_bg_f, gla_wg2_b, gla_bg_b, gla_out_norm,
           na_q_norm, na_k_norm, na_rpb, na_out_norm, mem_norm, mem_wkv, mem_q_norm, mem_k_norm,
           mem_out_norm, w_out, ffn_norm, ffn_w13, ffn_w2):
    B, S, D = x.shape
    depth = w_in.shape[0]
    T = B * S
    gla_dk = GLA_HEADS * GLA_HK
    gla_dv = GLA_HEADS * GLA_HV
    na_w = NA_HEADS * NA_HD
    mem_w = MEM_HEADS * MEM_HD
    gate0 = 2 * gla_dk + 2 * gla_dv
    gate1 = gate0 + 2 * GLA_RANK
    na_col0 = gate0
    mem_col0 = gate0 + 3 * na_w
    assert w_in.shape[2] - 2 * GLA_RANK == mem_col0 + mem_w and gla_dv + na_w + mem_w == w_out.shape[1]
    assert na_col0 % IN_PROJ_TN == 0 and 2 * na_w == IN_PROJ_TN

    w_in_t = jnp.swapaxes(w_in, 1, 2)
    wt_b, w_out_b, w13_b, w2_b = (w[0].astype(BF16) for w in (w_in_t, w_out, ffn_w13, ffn_w2))
    wkv_b = mem_wkv.astype(BF16)
    gqk = jnp.concatenate([jnp.tile(na_q_norm * (NA_HD ** -0.5), (1, NA_HEADS)),
                           jnp.tile(na_k_norm, (1, NA_HEADS))], axis=1)
    wb = natten_bias_tables(na_rpb)

    xf = x.reshape(T, D)
    k_m, v_m = mem_kv(mem.reshape(B * mem.shape[1], D), mem_norm[:, None], wkv_b, mem_k_norm[:, None])
    for l in range(depth):
        proj, gates, y_mem = in_proj(xf, attn_norm[l][None], wt_b, gqk[l][None], k_m, v_m,
                                     mem_q_norm[l][None], mem_out_norm[l][None], layer=l, batch=B,
                                     tm=512, sub=256, tn=IN_PROJ_TN, qk_tile=na_col0 // IN_PROJ_TN,
                                     gate0=gate0, ngate=gate1 - gate0)
        nxt = (lambda *ws: [(w, l + 1) for w in ws]) if l + 1 < depth else (lambda *ws: [])
        o_f, cast_f = gla_direction(proj, gates, gla_wg2_f[l], gla_bg_f[l][None], batch=B, tb=1024,
                                    reverse=False, cast_next=nxt(ffn_w2))
        y_gla, cast_b = gla_direction(proj, gates, gla_wg2_b[l], gla_bg_b[l][None], batch=B, tb=1024,
                                      reverse=True, o_fwd=o_f, out_gain=gla_out_norm[l][None],
                                      cast_next=nxt(w_out, w_in_t))
        y_na, cast_n = natten(proj, wb, na_out_norm[l][None], layer=l, batch=B, col0=na_col0,
                              cast_next=nxt(ffn_w13))
        xf, hf = out_proj(xf, y_gla, y_na, y_mem, w_out_b, ffn_norm[l][None], tm=512, sub=256)
        xf = ffn(xf, hf, w13_b, w2_b, tm=512, tf=512)
        if l + 1 < depth:
            (w2_b,), (w_out_b, wt_b), (w13_b,) = cast_f, cast_b, cast_n
    return xf.reshape(B, S, D)
```

```python
import functools

import jax
import jax.numpy as jnp
from jax import lax
from jax.experimental import pallas as pl
from jax.experimental.pallas import tpu as pltpu

F32 = jnp.float32
BF16 = jnp.bfloat16

RMS_EPS = 1e-6
MASK_VALUE = -1e30

GRID_W = 64
GLA_HEADS = 4
GLA_HK = 128
GLA_HV = 256
GLA_RANK = 16
GLA_TAU = 16.0
GLA_CHUNK = 64
NA_HD = 64
NA_HEADS = 8
NA_ROWS = 8
NA_COLS = 16
MEM_HEADS = 4
MEM_HD = 128

V7X_MXU_WIDTH = 256
V7X_VMEM_BYTES = 64 * 1024 * 1024
VMEM_LIMIT_BYTES = V7X_VMEM_BYTES - 8 * 1024 * 1024

NT_DIMS = (((1,), (1,)), ((), ()))


def _cparams(sem):
    return pltpu.CompilerParams(dimension_semantics=sem, vmem_limit_bytes=VMEM_LIMIT_BYTES)


def _rms(x, gain):
    ms = jnp.mean(x * x, axis=-1, keepdims=True)
    return (x * lax.rsqrt(ms + RMS_EPS)) * gain


def _head_pair_ones(width=2 * NA_HD):
    r = lax.broadcasted_iota(jnp.int32, (width, width), 0) // NA_HD
    c = lax.broadcasted_iota(jnp.int32, (width, width), 1) // NA_HD
    return jnp.where(r == c, 1.0, 0.0).astype(BF16)


def _head_pair_rms(x, gain, ones_bd):
    ms = jnp.dot((x * x).astype(BF16), ones_bd, preferred_element_type=F32) * (1.0 / NA_HD)
    return (x * lax.rsqrt(ms + RMS_EPS)) * gain


CAST_ROW_UNIT = 16


def _cast_streams(cast_next, steps, step_of):
    in_specs, out_specs, out_shapes, operands = [], [], [], []
    for w, layer in cast_next:
        _, rows, cols = w.shape
        units = rows // CAST_ROW_UNIT
        assert rows % CAST_ROW_UNIT == 0
        mult = min(d for d in range(1, units + 1) if units % d == 0 and units // d <= steps)
        nblk = units // mult
        per = steps // nblk

        def blk(*g, per=per, nblk=nblk):
            return jnp.minimum(step_of(*g) // per, nblk - 1)

        in_specs.append(pl.BlockSpec((None, mult * CAST_ROW_UNIT, cols),
                                     lambda *g, blk=blk, layer=layer: (layer, blk(*g), 0)))
        out_specs.append(pl.BlockSpec((mult * CAST_ROW_UNIT, cols), lambda *g, blk=blk: (blk(*g), 0)))
        out_shapes.append(jax.ShapeDtypeStruct((rows, cols), BF16))
        operands.append(w)
    return in_specs, out_specs, out_shapes, operands


def _with_cast_streams(body, n_in, n_out, n_cast):
    def kern(*refs):
        ins, rest = refs[:n_in], refs[n_in:]
        cast_in, rest = rest[:n_cast], rest[n_cast:]
        outs, rest = rest[:n_out], rest[n_out:]
        cast_out, scratch = rest[:n_cast], rest[n_cast:]
        body(*ins, *outs, *scratch)
        for src, dst in zip(cast_in, cast_out):
            dst[...] = src[...].astype(BF16)
    return kern


def _mem_attention(mq, km_ref, vm_ref, gq_ref, go_ref, y_ref, rows):
    heads = range(MEM_HEADS)
    cols = [slice(h * MEM_HD, (h + 1) * MEM_HD) for h in heads]
    q = [(_rms(mq[:, c], gq_ref[...]) * (MEM_HD ** -0.5)).astype(BF16) for c in cols]
    s = [lax.dot_general(q[h], km_ref[:, cols[h]], NT_DIMS, preferred_element_type=F32) for h in heads]
    p = [jnp.exp(sh - jnp.max(sh, axis=-1, keepdims=True)) for sh in s]
    l = [jnp.sum(ph, axis=-1, keepdims=True) for ph in p]
    o = [jnp.dot(p[h].astype(BF16), vm_ref[:, cols[h]], preferred_element_type=F32) / l[h] for h in heads]
    for h in heads:
        y_ref[rows, cols[h]] = _rms(o[h], go_ref[:, cols[h]]).astype(y_ref.dtype)


def _in_proj_kernel(x_ref, g_ref, wt_ref, gqk_ref, km_ref, vm_ref, gmq_ref, gmo_ref,
                    proj_ref, gate_ref, ymem_ref, *, sub, tn, qk_tile, gate0):
    tm = x_ref.shape[0]
    ngate = gate_ref.shape[1]
    lw = V7X_MXU_WIDTH
    ones_bd = _head_pair_ones(lw)
    n_proj = proj_ref.shape[1]
    ntile = (wt_ref.shape[0] - ngate) // tn
    for s in range(tm // sub):
        rows = slice(s * sub, (s + 1) * sub)
        xn = _rms(x_ref[rows, :], g_ref[...]).astype(BF16)
        gate_ref[rows, :] = lax.dot_general(xn, wt_ref[gate0:gate0 + ngate, :], NT_DIMS,
                                            preferred_element_type=F32)
        for n in [ntile - 1] + list(range(ntile - 1)):
            w0 = n * tn + (ngate if n * tn >= gate0 else 0)
            acc = lax.dot_general(xn, wt_ref[w0:w0 + tn, :], NT_DIMS, preferred_element_type=F32)
            keep = min(tn, n_proj - n * tn)
            if n == qk_tile:
                for c in range(tn // lw):
                    cols = slice(c * lw, (c + 1) * lw)
                    proj_ref[rows, n * tn + c * lw:n * tn + (c + 1) * lw] = _head_pair_rms(
                        acc[:, cols], gqk_ref[:, cols], ones_bd).astype(BF16)
            else:
                proj_ref[rows, n * tn:n * tn + keep] = acc[:, :keep].astype(BF16)
            if keep < tn:
                _mem_attention(acc[:, keep:], km_ref, vm_ref, gmq_ref, gmo_ref, ymem_ref, rows)


def in_proj(x, gain, w_t, gqk, k_m, v_m, gmq, gmo, *, layer, batch, tm, sub, tn, qk_tile, gate0, ngate,
            cast_next=()):
    T, D = x.shape
    n_in = w_t.shape[0]
    N = n_in - ngate
    G = ngate
    assert gate0 % tn == 0 and N % tn == 0
    mem_w = MEM_HEADS * MEM_HD
    n_mem = k_m.shape[1] // batch
    per_b = (T // batch) // tm
    args = [x, gain, w_t, gqk, k_m, v_m, gmq, gmo]
    c_in, c_out, c_shapes, c_args = _cast_streams(cast_next, T // tm, lambda i: i)
    kern = functools.partial(_in_proj_kernel, sub=sub, tn=tn, qk_tile=qk_tile, gate0=gate0)
    outs = pl.pallas_call(
        _with_cast_streams(kern, len(args), 3, len(c_args)),
        grid=(T // tm,),
        in_specs=[
            pl.BlockSpec((tm, D), lambda i: (i, 0)),
            pl.BlockSpec((1, D), lambda i: (0, 0)),
            pl.BlockSpec((n_in, D), lambda i: (0, 0), pipeline_mode=pl.Buffered(1)),
            pl.BlockSpec((1, tn), lambda i: (0, 0)),
            pl.BlockSpec((None, n_mem, mem_w), lambda i: (layer, i // per_b, 0)),
            pl.BlockSpec((None, n_mem, mem_w), lambda i: (layer, i // per_b, 0)),
            pl.BlockSpec((1, MEM_HD), lambda i: (0, 0)),
            pl.BlockSpec((1, mem_w), lambda i: (0, 0)),
        ] + c_in,
        out_specs=[
            pl.BlockSpec((tm, N - mem_w), lambda i: (i, 0)),
            pl.BlockSpec((tm, G), lambda i: (i, 0)),
            pl.BlockSpec((tm, mem_w), lambda i: (i, 0)),
        ] + c_out,
        out_shape=[jax.ShapeDtypeStruct((T, N - mem_w), BF16), jax.ShapeDtypeStruct((T, G), F32),
                   jax.ShapeDtypeStruct((T, mem_w), BF16)] + c_shapes,
        compiler_params=_cparams(("arbitrary",)),
        name="in_proj",
    )(*args, *c_args)
    return outs[0], outs[1], outs[2], outs[3:]


def _log_sigmoid(z):
    return jnp.minimum(z, 0.0) - jnp.log(1.0 + jnp.exp(-jnp.abs(z)))


def _gla_kernel(q_ref, k_ref, v_ref, gate_ref, wg_ref, bg_ref, *rest, reverse, final, gate_off):
    if final:
        r_ref, of_ref, gn_ref, o_ref, st_ref = rest
    else:
        o_ref, st_ref = rest
    C = GLA_CHUNK
    H, dk, dv = GLA_HEADS, GLA_HK, GLA_HV
    tb = q_ref.shape[0]
    nchunk = tb // C

    @pl.when(pl.program_id(1) == 0)
    def _():
        st_ref[...] = jnp.zeros_like(st_ref)

    g = gate_ref[:, gate_off:gate_off + GLA_RANK]
    z = jnp.dot(g.astype(BF16), wg_ref[...].astype(BF16), preferred_element_type=F32) + bg_ref[...]
    la = _log_sigmoid(z) * (1.0 / GLA_TAU)

    la_t = jnp.concatenate([la[c * C:(c + 1) * C, :] for c in range(nchunk)], axis=1)
    row = lax.broadcasted_iota(jnp.int32, (C, C), 0)
    col = lax.broadcasted_iota(jnp.int32, (C, C), 1)
    if reverse:
        tri = (col >= row)
        keep = col > row
    else:
        tri = (col <= row)
        keep = col <= row
    tri = jnp.where(tri, 1.0, 0.0).astype(BF16)
    la_hi = la_t.astype(BF16)
    la_lo = (la_t - la_hi.astype(F32)).astype(BF16)
    cum = (jnp.dot(tri, la_hi, preferred_element_type=F32)
           + jnp.dot(tri, la_lo, preferred_element_type=F32))

    order = range(nchunk - 1, -1, -1) if reverse else range(nchunk)
    heads = range(H)
    kcols = [slice(h * dk, (h + 1) * dk) for h in heads]
    vcols = [slice(h * dv, (h + 1) * dv) for h in heads]
    for c in order:
        rows = slice(c * C, (c + 1) * C)
        cum_c = cum[:, c * H * dk:(c + 1) * H * dk]
        last = cum_c[0:1, :] if reverse else cum_c[C - 1:C, :]
        qc = q_ref[rows, :].astype(F32) * (GLA_HK ** -0.5)
        kc = k_ref[rows, :].astype(F32)
        q_e = (qc * jnp.exp(cum_c)).astype(BF16)
        k_e = (kc * jnp.exp(-cum_c)).astype(BF16)
        k_end = (kc * jnp.exp(last - cum_c)).astype(BF16)
        decay = jnp.exp(last)
        vc = [v_ref[rows, vcols[h]] for h in heads]
        sc = [lax.dot_general(q_e[:, kcols[h]], k_e[:, kcols[h]], NT_DIMS, preferred_element_type=F32)
              for h in heads]
        sc = [jnp.where(keep, s, 0.0).astype(BF16) for s in sc]
        st = [st_ref[h] for h in heads]
        o = [jnp.dot(sc[h], vc[h], preferred_element_type=F32)
             + lax.dot_general(q_e[:, kcols[h]], st[h].astype(BF16), NT_DIMS, preferred_element_type=F32)
             for h in heads]
        for h in heads:
            st_ref[h] = (st[h] * decay[:, kcols[h]]
                         + jnp.dot(vc[h].T, k_end[:, kcols[h]], preferred_element_type=F32))
        for h in heads:
            if final:
                oh = o[h] + of_ref[rows, vcols[h]].astype(F32)
                y = _rms(oh, gn_ref[:, vcols[h]])
                r = r_ref[rows, vcols[h]].astype(F32)
                o_ref[rows, vcols[h]] = (y * (r * jax.nn.sigmoid(r))).astype(o_ref.dtype)
            else:
                o_ref[rows, vcols[h]] = o[h].astype(o_ref.dtype)


def gla_direction(proj, gates, wg2, bg, *, batch, tb, reverse, o_fwd=None, out_gain=None, cast_next=()):
    T = proj.shape[0]
    S = T // batch
    nb = S // tb
    final = o_fwd is not None
    H, dk, dv = GLA_HEADS, GLA_HK, GLA_HV
    wk, wv = H * dk, H * dv

    def tok(b, i):
        return b * nb + ((nb - 1 - i) if reverse else i)

    in_specs = [
        pl.BlockSpec((tb, wk), lambda b, i: (tok(b, i), 0)),
        pl.BlockSpec((tb, wk), lambda b, i: (tok(b, i), 1)),
        pl.BlockSpec((tb, wv), lambda b, i: (tok(b, i), (2 * wk) // wv)),
        pl.BlockSpec((tb, gates.shape[1]), lambda b, i: (tok(b, i), 0)),
        pl.BlockSpec((GLA_RANK, wk), lambda b, i: (0, 0)),
        pl.BlockSpec((1, wk), lambda b, i: (0, 0)),
    ]
    args = [proj, proj, proj, gates, wg2, bg]
    if final:
        in_specs += [
            pl.BlockSpec((tb, wv), lambda b, i: (tok(b, i), (2 * wk) // wv + 1)),
            pl.BlockSpec((tb, wv), lambda b, i: (tok(b, i), 0)),
            pl.BlockSpec((1, wv), lambda b, i: (0, 0)),
        ]
        args += [proj, o_fwd, out_gain]
    kern = functools.partial(_gla_kernel, reverse=reverse, final=final,
                             gate_off=GLA_RANK if reverse else 0)
    c_in, c_out, c_shapes, c_args = _cast_streams(cast_next, batch * nb, lambda b, i: b * nb + i)
    outs = pl.pallas_call(
        _with_cast_streams(kern, len(args), 1, len(c_args)),
        grid=(batch, nb),
        in_specs=in_specs + c_in,
        out_specs=[pl.BlockSpec((tb, wv), lambda b, i: (tok(b, i), 0))] + c_out,
        out_shape=[jax.ShapeDtypeStruct((T, wv), BF16)] + c_shapes,
        scratch_shapes=[pltpu.VMEM((H, dv, dk), F32)],
        compiler_params=_cparams(("arbitrary", "arbitrary")),
        name="gla_bwd" if reverse else "gla_fwd",
    )(*args, *c_args)
    return outs[0], outs[1:]


NA_QROWS = 8
NA_DR = 2 * NA_ROWS - 1
NA_DC = 2 * NA_COLS - 1


def _natten_bias_kernel(rpb_ref, o_ref):
    W = GRID_W
    base = (pl.program_id(0) * NA_HEADS + pl.program_id(1)) * (NA_DR * NA_DC)
    qc = lax.broadcasted_iota(jnp.int32, (W, 2 * W), 0)
    ln = lax.broadcasted_iota(jnp.int32, (W, 2 * W), 1)
    kc = ln & (W - 1)
    dc_idx = jnp.clip(kc - qc, -(NA_COLS - 1), NA_COLS - 1) + (NA_COLS - 1)
    cs = jnp.clip(qc - NA_COLS // 2, 0, W - NA_COLS)
    in_win = (kc >= cs) & (kc < cs + NA_COLS)
    left = ln < W
    toeplitz = []
    for dr in range(NA_DR):
        t = jnp.full((W, 2 * W), MASK_VALUE, F32)
        for dc in range(NA_DC):
            t = jnp.where(dc_idx == dc, rpb_ref[base + dr * NA_DC + dc], t)
        toeplitz.append(jnp.where(in_win, t, MASK_VALUE))
    for dr0 in range(NA_ROWS):
        for jp in range(NA_ROWS // 2):
            o_ref[dr0, :, jp * 2 * W:(jp + 1) * 2 * W] = jnp.where(
                left, toeplitz[dr0 + 2 * jp], toeplitz[dr0 + 2 * jp + 1])


def natten_bias_tables(rpb):
    L, H = rpb.shape[:2]
    W = GRID_W
    return pl.pallas_call(
        _natten_bias_kernel,
        grid=(L, H),
        in_specs=[pl.BlockSpec(memory_space=pltpu.SMEM)],
        out_specs=pl.BlockSpec((None, None, NA_ROWS, W, NA_ROWS * W), lambda l, h: (l, h, 0, 0, 0)),
        out_shape=jax.ShapeDtypeStruct((L, H, NA_ROWS, W, NA_ROWS * W), F32),
        compiler_params=_cparams(("parallel", "parallel")),
        name="natten_bias",
    )(rpb.reshape(-1))


def _natten_kernel(q_ref, kp_ref, kc_ref, kn_ref, vp_ref, vc_ref, vn_ref, wb_ref, go_ref, o_ref,
                   o_scr, *, rows):
    W = GRID_W
    nq = q_ref.shape[0]
    lw = 2 * NA_HD
    nwin = NA_ROWS * W

    def window(refs, start, cols):
        parts = []
        for b, ref in enumerate(refs):
            lo, hi = max(start, b * nq), min(start + nwin, (b + 1) * nq)
            if lo < hi:
                parts.append(ref[lo - b * nq:hi - b * nq, cols])
        return parts[0] if len(parts) == 1 else jnp.concatenate(parts, axis=0)

    k_refs = (kp_ref, kc_ref, kn_ref)
    v_refs = (vp_ref, vc_ref, vn_ref)
    head0 = lax.broadcasted_iota(jnp.int32, (1, lw), 1) < NA_HD
    blk = pl.program_id(1)
    nblk = rows // NA_QROWS

    def step(r0):
        for a in range(NA_QROWS):
            rs = min(max(r0 + a - NA_ROWS // 2, 0), rows - NA_ROWS)
            off = rs - (r0 - NA_QROWS)
            dr0 = rs - (r0 + a) + (NA_ROWS - 1)
            qrows = slice(a * W, (a + 1) * W)
            pairs = range(NA_HEADS // 2)
            cols = [slice(p * lw, (p + 1) * lw) for p in pairs]
            s = []
            for p in pairs:
                qa = q_ref[qrows, cols[p]]
                zero = jnp.zeros_like(qa)
                lhs = jnp.concatenate([jnp.where(head0, qa, zero), jnp.where(head0, zero, qa)], axis=0)
                sp = lax.dot_general(lhs, window(k_refs, off * W, cols[p]), NT_DIMS,
                                     preferred_element_type=F32)
                s.append(sp + jnp.concatenate([wb_ref[2 * p, dr0], wb_ref[2 * p + 1, dr0]], axis=0))
            m = [jnp.max(sp, axis=-1, keepdims=True) for sp in s]
            e = [jnp.exp(sp - mp) for sp, mp in zip(s, m)]
            l = [jnp.sum(ep, axis=-1, keepdims=True) for ep in e]
            r = [jnp.dot(e[p].astype(BF16), window(v_refs, off * W, cols[p]), preferred_element_type=F32)
                 for p in pairs]
            for p in pairs:
                rp = r[p] / l[p]
                o_scr[qrows, cols[p]] = jnp.where(head0, rp[0:W], rp[W:2 * W])

    pl.when(blk == 0)(lambda: step(0))
    pl.when(jnp.logical_and(blk > 0, blk < nblk - 1))(lambda: step(NA_QROWS))
    pl.when(blk == nblk - 1)(lambda: step(rows - NA_QROWS))

    gw = V7X_MXU_WIDTH
    ones_bd = _head_pair_ones(gw)
    for g in range(o_ref.shape[1] // gw):
        cols = slice(g * gw, (g + 1) * gw)
        o_ref[:, cols] = _head_pair_rms(o_scr[:, cols], go_ref[:, cols], ones_bd).astype(o_ref.dtype)


def natten(proj, wb, go, *, layer, batch, col0, cast_next=()):
    T = proj.shape[0]
    S = T // batch
    rows = S // GRID_W
    nq = NA_QROWS * GRID_W
    nblk = S // nq
    width = NA_HEADS * NA_HD
    cq = col0 // width

    def prev(i):
        return jnp.maximum(i - 1, 0)

    def nxt(i):
        return jnp.minimum(i + 1, nblk - 1)

    in_specs = [
        pl.BlockSpec((nq, width), lambda b, i: (b * nblk + i, cq)),
        pl.BlockSpec((nq, width), lambda b, i: (b * nblk + prev(i), cq + 1)),
        pl.BlockSpec((nq, width), lambda b, i: (b * nblk + i, cq + 1)),
        pl.BlockSpec((nq, width), lambda b, i: (b * nblk + nxt(i), cq + 1)),
        pl.BlockSpec((nq, width), lambda b, i: (b * nblk + prev(i), cq + 2)),
        pl.BlockSpec((nq, width), lambda b, i: (b * nblk + i, cq + 2)),
        pl.BlockSpec((nq, width), lambda b, i: (b * nblk + nxt(i), cq + 2)),
        pl.BlockSpec((None,) + wb.shape[1:], lambda b, i: (layer, 0, 0, 0, 0), pipeline_mode=pl.Buffered(1)),
        pl.BlockSpec((1, width), lambda b, i: (0, 0)),
    ]
    args = [proj, proj, proj, proj, proj, proj, proj, wb, go]
    c_in, c_out, c_shapes, c_args = _cast_streams(cast_next, batch * nblk, lambda b, i: b * nblk + i)
    outs = pl.pallas_call(
        _with_cast_streams(functools.partial(_natten_kernel, rows=rows), len(args), 1, len(c_args)),
        grid=(batch, nblk),
        in_specs=in_specs + c_in,
        out_specs=[pl.BlockSpec((nq, width), lambda b, i: (b * nblk + i, 0))] + c_out,
        out_shape=[jax.ShapeDtypeStruct((T, width), BF16)] + c_shapes,
        scratch_shapes=[pltpu.VMEM((nq, width), F32)],
        compiler_params=_cparams(("arbitrary", "arbitrary")),
        name="natten",
    )(*args, *c_args)
    return outs[0], outs[1:]


def _mem_kv_kernel(mem_ref, g_ref, w_ref, gk_ref, k_ref, v_ref):
    mn = _rms(mem_ref[...], g_ref[...]).astype(BF16)
    kv = jnp.dot(mn, w_ref[...], preferred_element_type=F32)
    width = k_ref.shape[1]
    for h in range(MEM_HEADS):
        cols = slice(h * MEM_HD, (h + 1) * MEM_HD)
        k_ref[:, cols] = _rms(kv[:, cols], gk_ref[...]).astype(BF16)
    v_ref[...] = kv[:, width:].astype(BF16)


def mem_kv(mem2d, gain, wkv, gk):
    n, D = mem2d.shape
    L = wkv.shape[0]
    width = wkv.shape[2] // 2
    return pl.pallas_call(
        _mem_kv_kernel,
        grid=(L,),
        in_specs=[
            pl.BlockSpec((n, D), lambda l: (0, 0)),
            pl.BlockSpec((None, 1, D), lambda l: (l, 0, 0)),
            pl.BlockSpec((None, D, 2 * width), lambda l: (l, 0, 0)),
            pl.BlockSpec((None, 1, MEM_HD), lambda l: (l, 0, 0)),
        ],
        out_specs=[pl.BlockSpec((None, n, width), lambda l: (l, 0, 0)),
                   pl.BlockSpec((None, n, width), lambda l: (l, 0, 0))],
        out_shape=[jax.ShapeDtypeStruct((L, n, width), BF16), jax.ShapeDtypeStruct((L, n, width), BF16)],
        compiler_params=_cparams(("parallel",)),
        name="mem_kv",
    )(mem2d, gain, wkv, gk)


def _out_proj_kernel(x_ref, ya_ref, yb_ref, yc_ref, w_ref, g_ref, o_ref, h_ref, *, sub):
    tm = x_ref.shape[0]
    wa, wb = ya_ref.shape[1], yb_ref.shape[1]
    for s in range(tm // sub):
        rows = slice(s * sub, (s + 1) * sub)
        acc = x_ref[rows, :] + jnp.dot(ya_ref[rows, :], w_ref[0:wa, :], preferred_element_type=F32)
        acc += jnp.dot(yb_ref[rows, :], w_ref[wa:wa + wb, :], preferred_element_type=F32)
        acc += jnp.dot(yc_ref[rows, :], w_ref[wa + wb:, :], preferred_element_type=F32)
        o_ref[rows, :] = acc
        h_ref[rows, :] = _rms(acc, g_ref[...]).astype(BF16)


def out_proj(x, y_gla, y_na, y_mem, w_out, gain, *, tm, sub):
    T, D = x.shape
    wa, wb, wc = y_gla.shape[1], y_na.shape[1], y_mem.shape[1]
    return pl.pallas_call(
        functools.partial(_out_proj_kernel, sub=sub),
        grid=(T // tm,),
        in_specs=[
            pl.BlockSpec((tm, D), lambda i: (i, 0)),
            pl.BlockSpec((tm, wa), lambda i: (i, 0)),
            pl.BlockSpec((tm, wb), lambda i: (i, 0)),
            pl.BlockSpec((tm, wc), lambda i: (i, 0)),
            pl.BlockSpec((wa + wb + wc, D), lambda i: (0, 0), pipeline_mode=pl.Buffered(1)),
            pl.BlockSpec((1, D), lambda i: (0, 0)),
        ],
        out_specs=[pl.BlockSpec((tm, D), lambda i: (i, 0)), pl.BlockSpec((tm, D), lambda i: (i, 0))],
        out_shape=[jax.ShapeDtypeStruct((T, D), F32), jax.ShapeDtypeStruct((T, D), BF16)],
        compiler_params=_cparams(("parallel",)),
        name="out_proj",
    )(x, y_gla, y_na, y_mem, w_out, gain)


def _ffn_kernel(x_ref, h_ref, w1_ref, w3_ref, w2_ref, o_ref):
    @pl.when(pl.program_id(1) == 0)
    def _():
        o_ref[...] = x_ref[...]

    h = h_ref[...]
    gate = jnp.dot(h, w1_ref[...], preferred_element_type=F32)
    up = jnp.dot(h, w3_ref[...], preferred_element_type=F32)
    act = (gate * jax.nn.sigmoid(gate) * up).astype(BF16)
    o_ref[...] += jnp.dot(act, w2_ref[...], preferred_element_type=F32)


def ffn(x, h, w13, w2, *, tm, tf):
    T, D = x.shape
    F = w2.shape[0]
    nf = F // tf
    return pl.pallas_call(
        _ffn_kernel,
        grid=(T // tm, nf),
        in_specs=[
            pl.BlockSpec((tm, D), lambda i, f: (i, 0)),
            pl.BlockSpec((tm, D), lambda i, f: (i, 0)),
            pl.BlockSpec((D, tf), lambda i, f: (0, f)),
            pl.BlockSpec((D, tf), lambda i, f: (0, nf + f)),
            pl.BlockSpec((tf, D), lambda i, f: (f, 0)),
        ],
        out_specs=pl.BlockSpec((tm, D), lambda i, f: (i, 0)),
        out_shape=jax.ShapeDtypeStruct((T, D), F32),
        compiler_params=_cparams(("parallel", "arbitrary")),
        name="ffn",
    )(x, h, w13, w13, w2)


IN_PROJ_TN = 1024


def kernel(x, mem, attn_norm, w_in, gla_wg2_f, gla_bg_f, gla_wg2_b, gla_bg_b, gla_out_norm,
           na_q_norm, na_k_norm, na_rpb, na_out_norm, mem_norm, mem_wkv, mem_q_norm, mem_k_norm,
           mem_out_norm, w_out, ffn_norm, ffn_w13, ffn_w2):
    B, S, D = x.shape
    depth = w_in.shape[0]
    T = B * S
    gla_dk = GLA_HEADS * GLA_HK
    gla_dv = GLA_HEADS * GLA_HV
    na_w = NA_HEADS * NA_HD
    mem_w = MEM_HEADS * MEM_HD
    gate0 = 2 * gla_dk + 2 * gla_dv
    gate1 = gate0 + 2 * GLA_RANK
    na_col0 = gate0
    mem_col0 = gate0 + 3 * na_w
    assert w_in.shape[2] - 2 * GLA_RANK == mem_col0 + mem_w and gla_dv + na_w + mem_w == w_out.shape[1]
    assert na_col0 % IN_PROJ_TN == 0 and 2 * na_w == IN_PROJ_TN

    w_in_t = jnp.swapaxes(w_in, 1, 2)
    wt_b, w_out_b, w13_b, w2_b = (w[0].astype(BF16) for w in (w_in_t, w_out, ffn_w13, ffn_w2))
    wkv_b = mem_wkv.astype(BF16)
    gqk = jnp.concatenate([jnp.tile(na_q_norm * (NA_HD ** -0.5), (1, NA_HEADS)),
                           jnp.tile(na_k_norm, (1, NA_HEADS))], axis=1)
    wb = natten_bias_tables(na_rpb)

    xf = x.reshape(T, D)
    k_m, v_m = mem_kv(mem.reshape(B * mem.shape[1], D), mem_norm[:, None], wkv_b, mem_k_norm[:, None])
    for l in range(depth):
        nxt = (lambda *ws: [(w, l + 1) for w in ws]) if l + 1 < depth else (lambda *ws: [])
        proj, gates, y_mem, cast_p = in_proj(
            xf, attn_norm[l][None], wt_b, gqk[l][None], k_m, v_m, mem_q_norm[l][None], mem_out_norm[l][None],
            layer=l, batch=B, tm=512, sub=256, tn=IN_PROJ_TN, qk_tile=na_col0 // IN_PROJ_TN,
            gate0=gate0, ngate=gate1 - gate0, cast_next=nxt(w_out, w_in_t))
        o_f, cast_f = gla_direction(proj, gates, gla_wg2_f[l], gla_bg_f[l][None], batch=B, tb=1024,
                                    reverse=False, cast_next=nxt(ffn_w2))
        y_gla, _ = gla_direction(proj, gates, gla_wg2_b[l], gla_bg_b[l][None], batch=B, tb=1024,
                                 reverse=True, o_fwd=o_f, out_gain=gla_out_norm[l][None])
        y_na, cast_n = natten(proj, wb, na_out_norm[l][None], layer=l, batch=B, col0=na_col0,
                              cast_next=nxt(ffn_w13))
        xf, hf = out_proj(xf, y_gla, y_na, y_mem, w_out_b, ffn_norm[l][None], tm=512, sub=256)
        xf = ffn(xf, hf, w13_b, w2_b, tm=512, tf=512)
        if l + 1 < depth:
            (w_out_b, wt_b), (w2_b,), (w13_b,) = cast_p, cast_f, cast_n
    return xf.reshape(B, S, D)
```

```python
import functools

import jax
import jax.numpy as jnp
from jax import lax
from jax.experimental import pallas as pl
from jax.experimental.pallas import tpu as pltpu

F32 = jnp.float32
BF16 = jnp.bfloat16

RMS_EPS = 1e-6
MASK_VALUE = -1e30

GRID_W = 64
GLA_HEADS = 4
GLA_HK = 128
GLA_HV = 256
GLA_RANK = 16
GLA_TAU = 16.0
GLA_CHUNK = 64
NA_HD = 64
NA_HEADS = 8
NA_ROWS = 8
NA_COLS = 16
MEM_HEADS = 4
MEM_HD = 128

V7X_MXU_WIDTH = 256
V7X_VMEM_BYTES = 64 * 1024 * 1024
VMEM_LIMIT_BYTES = V7X_VMEM_BYTES - 2 * 1024 * 1024

NT_DIMS = (((1,), (1,)), ((), ()))


def _cparams(sem):
    return pltpu.CompilerParams(dimension_semantics=sem, vmem_limit_bytes=VMEM_LIMIT_BYTES)


def _rms(x, gain):
    ms = jnp.mean(x * x, axis=-1, keepdims=True)
    return (x * lax.rsqrt(ms + RMS_EPS)) * gain


def _head_pair_ones(width=2 * NA_HD):
    r = lax.broadcasted_iota(jnp.int32, (width, width), 0) // NA_HD
    c = lax.broadcasted_iota(jnp.int32, (width, width), 1) // NA_HD
    return jnp.where(r == c, 1.0, 0.0).astype(BF16)


def _head_pair_rms(x, gain, ones_bd):
    ms = jnp.dot((x * x).astype(BF16), ones_bd, preferred_element_type=F32) * (1.0 / NA_HD)
    return (x * lax.rsqrt(ms + RMS_EPS)) * gain


CAST_ROW_UNIT = 16


def _cast_streams(cast_next, steps, step_of):
    in_specs, out_specs, out_shapes, operands = [], [], [], []
    for w, layer in cast_next:
        _, rows, cols = w.shape
        units = rows // CAST_ROW_UNIT
        assert rows % CAST_ROW_UNIT == 0
        mult = min(d for d in range(1, units + 1) if units % d == 0 and units // d <= steps)
        nblk = units // mult
        per = steps // nblk

        def blk(*g, per=per, nblk=nblk):
            return jnp.minimum(step_of(*g) // per, nblk - 1)

        in_specs.append(pl.BlockSpec((None, mult * CAST_ROW_UNIT, cols),
                                     lambda *g, blk=blk, layer=layer: (layer, blk(*g), 0)))
        out_specs.append(pl.BlockSpec((mult * CAST_ROW_UNIT, cols), lambda *g, blk=blk: (blk(*g), 0)))
        out_shapes.append(jax.ShapeDtypeStruct((rows, cols), BF16))
        operands.append(w)
    return in_specs, out_specs, out_shapes, operands


def _with_cast_streams(body, n_in, n_out, n_cast):
    def kern(*refs):
        ins, rest = refs[:n_in], refs[n_in:]
        cast_in, rest = rest[:n_cast], rest[n_cast:]
        outs, rest = rest[:n_out], rest[n_out:]
        cast_out, scratch = rest[:n_cast], rest[n_cast:]
        body(*ins, *outs, *scratch)
        for src, dst in zip(cast_in, cast_out):
            dst[...] = src[...].astype(BF16)
    return kern


def _mem_attention(mq, km_ref, vm_ref, gq_ref, go_ref, y_ref, rows):
    heads = range(MEM_HEADS)
    cols = [slice(h * MEM_HD, (h + 1) * MEM_HD) for h in heads]
    q = [(_rms(mq[:, c], gq_ref[...]) * (MEM_HD ** -0.5)).astype(BF16) for c in cols]
    s = [lax.dot_general(q[h], km_ref[:, cols[h]], NT_DIMS, preferred_element_type=F32) for h in heads]
    p = [jnp.exp(sh - jnp.max(sh, axis=-1, keepdims=True)) for sh in s]
    l = [jnp.sum(ph, axis=-1, keepdims=True) for ph in p]
    o = [jnp.dot(p[h].astype(BF16), vm_ref[:, cols[h]], preferred_element_type=F32) / l[h] for h in heads]
    for h in heads:
        y_ref[rows, cols[h]] = _rms(o[h], go_ref[:, cols[h]]).astype(y_ref.dtype)


def _in_proj_kernel(x_ref, g_ref, wt_ref, gqk_ref, km_ref, vm_ref, gmq_ref, gmo_ref,
                    proj_ref, gate_ref, ymem_ref, *, sub, tn, qk_tile, gate0):
    tm = x_ref.shape[0]
    ngate = gate_ref.shape[1]
    lw = V7X_MXU_WIDTH
    ones_bd = _head_pair_ones(lw)
    n_proj = proj_ref.shape[1]
    ntile = (wt_ref.shape[0] - ngate) // tn
    for s in range(tm // sub):
        rows = slice(s * sub, (s + 1) * sub)
        xn = _rms(x_ref[rows, :], g_ref[...]).astype(BF16)
        gate_ref[rows, :] = lax.dot_general(xn, wt_ref[gate0:gate0 + ngate, :], NT_DIMS,
                                            preferred_element_type=F32)
        for n in [ntile - 1] + list(range(ntile - 1)):
            w0 = n * tn + (ngate if n * tn >= gate0 else 0)
            acc = lax.dot_general(xn, wt_ref[w0:w0 + tn, :], NT_DIMS, preferred_element_type=F32)
            keep = min(tn, n_proj - n * tn)
            if n == qk_tile:
                for c in range(tn // lw):
                    cols = slice(c * lw, (c + 1) * lw)
                    proj_ref[rows, n * tn + c * lw:n * tn + (c + 1) * lw] = _head_pair_rms(
                        acc[:, cols], gqk_ref[:, cols], ones_bd).astype(BF16)
            else:
                proj_ref[rows, n * tn:n * tn + keep] = acc[:, :keep].astype(BF16)
            if keep < tn:
                _mem_attention(acc[:, keep:], km_ref, vm_ref, gmq_ref, gmo_ref, ymem_ref, rows)


def in_proj(x, gain, w_t, gqk, k_m, v_m, gmq, gmo, *, layer, batch, tm, sub, tn, qk_tile, gate0, ngate,
            cast_next=()):
    T, D = x.shape
    n_in = w_t.shape[0]
    N = n_in - ngate
    G = ngate
    assert gate0 % tn == 0 and N % tn == 0
    mem_w = MEM_HEADS * MEM_HD
    n_mem = k_m.shape[1] // batch
    per_b = (T // batch) // tm
    args = [x, gain, w_t, gqk, k_m, v_m, gmq, gmo]
    c_in, c_out, c_shapes, c_args = _cast_streams(cast_next, T // tm, lambda i: i)
    kern = functools.partial(_in_proj_kernel, sub=sub, tn=tn, qk_tile=qk_tile, gate0=gate0)
    outs = pl.pallas_call(
        _with_cast_streams(kern, len(args), 3, len(c_args)),
        grid=(T // tm,),
        in_specs=[
            pl.BlockSpec((tm, D), lambda i: (i, 0)),
            pl.BlockSpec((1, D), lambda i: (0, 0)),
            pl.BlockSpec((n_in, D), lambda i: (0, 0), pipeline_mode=pl.Buffered(1)),
            pl.BlockSpec((1, tn), lambda i: (0, 0)),
            pl.BlockSpec((None, n_mem, mem_w), lambda i: (layer, i // per_b, 0)),
            pl.BlockSpec((None, n_mem, mem_w), lambda i: (layer, i // per_b, 0)),
            pl.BlockSpec((1, MEM_HD), lambda i: (0, 0)),
            pl.BlockSpec((1, mem_w), lambda i: (0, 0)),
        ] + c_in,
        out_specs=[
            pl.BlockSpec((tm, N - mem_w), lambda i: (i, 0)),
            pl.BlockSpec((tm, G), lambda i: (i, 0)),
            pl.BlockSpec((tm, mem_w), lambda i: (i, 0)),
        ] + c_out,
        out_shape=[jax.ShapeDtypeStruct((T, N - mem_w), BF16), jax.ShapeDtypeStruct((T, G), F32),
                   jax.ShapeDtypeStruct((T, mem_w), BF16)] + c_shapes,
        compiler_params=_cparams(("arbitrary",)),
        name="in_proj",
    )(*args, *c_args)
    return outs[0], outs[1], outs[2], outs[3:]


def _log_sigmoid(z):
    return jnp.minimum(z, 0.0) - jnp.log(1.0 + jnp.exp(-jnp.abs(z)))


def _gla_kernel(q_ref, k_ref, v_ref, gate_ref, wg_ref, bg_ref, *rest, reverse, final, gate_off):
    if final:
        r_ref, of_ref, gn_ref, o_ref, st_ref = rest
    else:
        o_ref, st_ref = rest
    C = GLA_CHUNK
    H, dk, dv = GLA_HEADS, GLA_HK, GLA_HV
    tb = q_ref.shape[0]
    nchunk = tb // C

    @pl.when(pl.program_id(1) == 0)
    def _():
        st_ref[...] = jnp.zeros_like(st_ref)

    g = gate_ref[:, gate_off:gate_off + GLA_RANK]
    z = jnp.dot(g.astype(BF16), wg_ref[...].astype(BF16), preferred_element_type=F32) + bg_ref[...]
    la = _log_sigmoid(z) * (1.0 / GLA_TAU)

    la_t = jnp.concatenate([la[c * C:(c + 1) * C, :] for c in range(nchunk)], axis=1)
    row = lax.broadcasted_iota(jnp.int32, (C, C), 0)
    col = lax.broadcasted_iota(jnp.int32, (C, C), 1)
    if reverse:
        tri = (col >= row)
        keep = col > row
    else:
        tri = (col <= row)
        keep = col <= row
    tri = jnp.where(tri, 1.0, 0.0).astype(BF16)
    la_hi = la_t.astype(BF16)
    la_lo = (la_t - la_hi.astype(F32)).astype(BF16)
    cum = (jnp.dot(tri, la_hi, preferred_element_type=F32)
           + jnp.dot(tri, la_lo, preferred_element_type=F32))

    order = range(nchunk - 1, -1, -1) if reverse else range(nchunk)
    heads = range(H)
    kcols = [slice(h * dk, (h + 1) * dk) for h in heads]
    vcols = [slice(h * dv, (h + 1) * dv) for h in heads]
    for c in order:
        rows = slice(c * C, (c + 1) * C)
        cum_c = cum[:, c * H * dk:(c + 1) * H * dk]
        last = cum_c[0:1, :] if reverse else cum_c[C - 1:C, :]
        qc = q_ref[rows, :].astype(F32) * (GLA_HK ** -0.5)
        kc = k_ref[rows, :].astype(F32)
        q_e = (qc * jnp.exp(cum_c)).astype(BF16)
        k_e = (kc * jnp.exp(-cum_c)).astype(BF16)
        k_end = (kc * jnp.exp(last - cum_c)).astype(BF16)
        decay = jnp.exp(last)
        vc = [v_ref[rows, vcols[h]] for h in heads]
        sc = [lax.dot_general(q_e[:, kcols[h]], k_e[:, kcols[h]], NT_DIMS, preferred_element_type=F32)
              for h in heads]
        sc = [jnp.where(keep, s, 0.0).astype(BF16) for s in sc]
        st = [st_ref[h] for h in heads]
        o = [jnp.dot(sc[h], vc[h], preferred_element_type=F32)
             + lax.dot_general(q_e[:, kcols[h]], st[h].astype(BF16), NT_DIMS, preferred_element_type=F32)
             for h in heads]
        for h in heads:
            st_ref[h] = (st[h] * decay[:, kcols[h]]
                         + jnp.dot(vc[h].T, k_end[:, kcols[h]], preferred_element_type=F32))
        for h in heads:
            if final:
                oh = o[h] + of_ref[rows, vcols[h]].astype(F32)
                y = _rms(oh, gn_ref[:, vcols[h]])
                r = r_ref[rows, vcols[h]].astype(F32)
                o_ref[rows, vcols[h]] = (y * (r * jax.nn.sigmoid(r))).astype(o_ref.dtype)
            else:
                o_ref[rows, vcols[h]] = o[h].astype(o_ref.dtype)


def gla_direction(proj, gates, wg2, bg, *, batch, tb, reverse, o_fwd=None, out_gain=None, cast_next=()):
    T = proj.shape[0]
    S = T // batch
    nb = S // tb
    final = o_fwd is not None
    H, dk, dv = GLA_HEADS, GLA_HK, GLA_HV
    wk, wv = H * dk, H * dv

    def tok(b, i):
        return b * nb + ((nb - 1 - i) if reverse else i)

    in_specs = [
        pl.BlockSpec((tb, wk), lambda b, i: (tok(b, i), 0)),
        pl.BlockSpec((tb, wk), lambda b, i: (tok(b, i), 1)),
        pl.BlockSpec((tb, wv), lambda b, i: (tok(b, i), (2 * wk) // wv)),
        pl.BlockSpec((tb, gates.shape[1]), lambda b, i: (tok(b, i), 0)),
        pl.BlockSpec((GLA_RANK, wk), lambda b, i: (0, 0)),
        pl.BlockSpec((1, wk), lambda b, i: (0, 0)),
    ]
    args = [proj, proj, proj, gates, wg2, bg]
    if final:
        in_specs += [
            pl.BlockSpec((tb, wv), lambda b, i: (tok(b, i), (2 * wk) // wv + 1)),
            pl.BlockSpec((tb, wv), lambda b, i: (tok(b, i), 0)),
            pl.BlockSpec((1, wv), lambda b, i: (0, 0)),
        ]
        args += [proj, o_fwd, out_gain]
    kern = functools.partial(_gla_kernel, reverse=reverse, final=final,
                             gate_off=GLA_RANK if reverse else 0)
    c_in, c_out, c_shapes, c_args = _cast_streams(cast_next, batch * nb, lambda b, i: b * nb + i)
    outs = pl.pallas_call(
        _with_cast_streams(kern, len(args), 1, len(c_args)),
        grid=(batch, nb),
        in_specs=in_specs + c_in,
        out_specs=[pl.BlockSpec((tb, wv), lambda b, i: (tok(b, i), 0))] + c_out,
        out_shape=[jax.ShapeDtypeStruct((T, wv), BF16)] + c_shapes,
        scratch_shapes=[pltpu.VMEM((H, dv, dk), F32)],
        compiler_params=_cparams(("arbitrary", "arbitrary")),
        name="gla_bwd" if reverse else "gla_fwd",
    )(*args, *c_args)
    return outs[0], outs[1:]


NA_QROWS = 8
NA_DR = 2 * NA_ROWS - 1
NA_DC = 2 * NA_COLS - 1


def _natten_bias_kernel(rpb_ref, o_ref):
    W = GRID_W
    qc = lax.broadcasted_iota(jnp.int32, (W, 2 * W), 0)
    ln = lax.broadcasted_iota(jnp.int32, (W, 2 * W), 1)
    kc = ln & (W - 1)
    dc_idx = jnp.clip(kc - qc, -(NA_COLS - 1), NA_COLS - 1) + (NA_COLS - 1)
    cs = jnp.clip(qc - NA_COLS // 2, 0, W - NA_COLS)
    in_win = (kc >= cs) & (kc < cs + NA_COLS)
    left = ln < W
    toeplitz = []
    for dr in range(NA_DR):
        row = jnp.broadcast_to(rpb_ref[dr:dr + 1, :], (W, 2 * W))
        t = jnp.take_along_axis(row, dc_idx, axis=1)
        toeplitz.append(jnp.where(in_win, t, MASK_VALUE))
    for dr0 in range(NA_ROWS):
        for jp in range(NA_ROWS // 2):
            o_ref[dr0, :, jp * 2 * W:(jp + 1) * 2 * W] = jnp.where(
                left, toeplitz[dr0 + 2 * jp], toeplitz[dr0 + 2 * jp + 1])


def natten_bias_tables(rpb):
    L, H = rpb.shape[:2]
    W = GRID_W
    rpb = jnp.pad(rpb, ((0, 0), (0, 0), (0, 0), (0, 2 * W - NA_DC)))
    return pl.pallas_call(
        _natten_bias_kernel,
        grid=(L, H),
        in_specs=[pl.BlockSpec((None, None, NA_DR, 2 * W), lambda l, h: (l, h, 0, 0))],
        out_specs=pl.BlockSpec((None, None, NA_ROWS, W, NA_ROWS * W), lambda l, h: (l, h, 0, 0, 0)),
        out_shape=jax.ShapeDtypeStruct((L, H, NA_ROWS, W, NA_ROWS * W), F32),
        compiler_params=_cparams(("parallel", "parallel")),
        name="natten_bias",
    )(rpb)


def _natten_kernel(q_ref, kp_ref, kc_ref, kn_ref, vp_ref, vc_ref, vn_ref, wb_ref, go_ref, o_ref,
                   o_scr, *, rows):
    W = GRID_W
    nq = q_ref.shape[0]
    lw = 2 * NA_HD
    nwin = NA_ROWS * W

    def window(refs, start, cols):
        parts = []
        for b, ref in enumerate(refs):
            lo, hi = max(start, b * nq), min(start + nwin, (b + 1) * nq)
            if lo < hi:
                parts.append(ref[lo - b * nq:hi - b * nq, cols])
        return parts[0] if len(parts) == 1 else jnp.concatenate(parts, axis=0)

    k_refs = (kp_ref, kc_ref, kn_ref)
    v_refs = (vp_ref, vc_ref, vn_ref)
    head0 = lax.broadcasted_iota(jnp.int32, (1, lw), 1) < NA_HD
    blk = pl.program_id(1)
    nblk = rows // NA_QROWS

    def step(r0):
        for a in range(NA_QROWS):
            rs = min(max(r0 + a - NA_ROWS // 2, 0), rows - NA_ROWS)
            off = rs - (r0 - NA_QROWS)
            dr0 = rs - (r0 + a) + (NA_ROWS - 1)
            qrows = slice(a * W, (a + 1) * W)
            pairs = range(NA_HEADS // 2)
            cols = [slice(p * lw, (p + 1) * lw) for p in pairs]
            s = []
            for p in pairs:
                qa = q_ref[qrows, cols[p]]
                zero = jnp.zeros_like(qa)
                lhs = jnp.concatenate([jnp.where(head0, qa, zero), jnp.where(head0, zero, qa)], axis=0)
                sp = lax.dot_general(lhs, window(k_refs, off * W, cols[p]), NT_DIMS,
                                     preferred_element_type=F32)
                s.append(sp + jnp.concatenate([wb_ref[2 * p, dr0], wb_ref[2 * p + 1, dr0]], axis=0))
            m = [jnp.max(sp, axis=-1, keepdims=True) for sp in s]
            e = [jnp.exp(sp - mp) for sp, mp in zip(s, m)]
            l = [jnp.sum(ep, axis=-1, keepdims=True) for ep in e]
            r = [jnp.dot(e[p].astype(BF16), window(v_refs, off * W, cols[p]), preferred_element_type=F32)
                 for p in pairs]
            for p in pairs:
                rp = r[p] / l[p]
                o_scr[qrows, cols[p]] = jnp.where(head0, rp[0:W], rp[W:2 * W])

    pl.when(blk == 0)(lambda: step(0))
    pl.when(jnp.logical_and(blk > 0, blk < nblk - 1))(lambda: step(NA_QROWS))
    pl.when(blk == nblk - 1)(lambda: step(rows - NA_QROWS))

    gw = V7X_MXU_WIDTH
    ones_bd = _head_pair_ones(gw)
    for g in range(o_ref.shape[1] // gw):
        cols = slice(g * gw, (g + 1) * gw)
        o_ref[:, cols] = _head_pair_rms(o_scr[:, cols], go_ref[:, cols], ones_bd).astype(o_ref.dtype)


def natten(proj, wb, go, *, layer, batch, col0, cast_next=()):
    T = proj.shape[0]
    S = T // batch
    rows = S // GRID_W
    nq = NA_QROWS * GRID_W
    nblk = S // nq
    width = NA_HEADS * NA_HD
    cq = col0 // width

    def prev(i):
        return jnp.maximum(i - 1, 0)

    def nxt(i):
        return jnp.minimum(i + 1, nblk - 1)

    in_specs = [
        pl.BlockSpec((nq, width), lambda b, i: (b * nblk + i, cq)),
        pl.BlockSpec((nq, width), lambda b, i: (b * nblk + prev(i), cq + 1)),
        pl.BlockSpec((nq, width), lambda b, i: (b * nblk + i, cq + 1)),
        pl.BlockSpec((nq, width), lambda b, i: (b * nblk + nxt(i), cq + 1)),
        pl.BlockSpec((nq, width), lambda b, i: (b * nblk + prev(i), cq + 2)),
        pl.BlockSpec((nq, width), lambda b, i: (b * nblk + i, cq + 2)),
        pl.BlockSpec((nq, width), lambda b, i: (b * nblk + nxt(i), cq + 2)),
        pl.BlockSpec((None,) + wb.shape[1:], lambda b, i: (layer, 0, 0, 0, 0), pipeline_mode=pl.Buffered(1)),
        pl.BlockSpec((1, width), lambda b, i: (0, 0)),
    ]
    args = [proj, proj, proj, proj, proj, proj, proj, wb, go]
    c_in, c_out, c_shapes, c_args = _cast_streams(cast_next, batch * nblk, lambda b, i: b * nblk + i)
    outs = pl.pallas_call(
        _with_cast_streams(functools.partial(_natten_kernel, rows=rows), len(args), 1, len(c_args)),
        grid=(batch, nblk),
        in_specs=in_specs + c_in,
        out_specs=[pl.BlockSpec((nq, width), lambda b, i: (b * nblk + i, 0))] + c_out,
        out_shape=[jax.ShapeDtypeStruct((T, width), BF16)] + c_shapes,
        scratch_shapes=[pltpu.VMEM((nq, width), F32)],
        compiler_params=_cparams(("arbitrary", "arbitrary")),
        name="natten",
    )(*args, *c_args)
    return outs[0], outs[1:]


def _mem_kv_kernel(mem_ref, g_ref, w_ref, gk_ref, k_ref, v_ref):
    mn = _rms(mem_ref[...], g_ref[...]).astype(BF16)
    kv = jnp.dot(mn, w_ref[...].astype(BF16), preferred_element_type=F32)
    width = k_ref.shape[1]
    for h in range(MEM_HEADS):
        cols = slice(h * MEM_HD, (h + 1) * MEM_HD)
        k_ref[:, cols] = _rms(kv[:, cols], gk_ref[...]).astype(BF16)
    v_ref[...] = kv[:, width:].astype(BF16)


def mem_kv(mem2d, gain, wkv, gk):
    n, D = mem2d.shape
    L = wkv.shape[0]
    width = wkv.shape[2] // 2
    return pl.pallas_call(
        _mem_kv_kernel,
        grid=(L,),
        in_specs=[
            pl.BlockSpec((n, D), lambda l: (0, 0)),
            pl.BlockSpec((None, 1, D), lambda l: (l, 0, 0)),
            pl.BlockSpec((None, D, 2 * width), lambda l: (l, 0, 0)),
            pl.BlockSpec((None, 1, MEM_HD), lambda l: (l, 0, 0)),
        ],
        out_specs=[pl.BlockSpec((None, n, width), lambda l: (l, 0, 0)),
                   pl.BlockSpec((None, n, width), lambda l: (l, 0, 0))],
        out_shape=[jax.ShapeDtypeStruct((L, n, width), BF16), jax.ShapeDtypeStruct((L, n, width), BF16)],
        compiler_params=_cparams(("parallel",)),
        name="mem_kv",
    )(mem2d, gain, wkv, gk)


def _out_proj_kernel(x_ref, ya_ref, yb_ref, yc_ref, w_ref, g_ref, o_ref, h_ref, *, sub):
    tm = x_ref.shape[0]
    wa, wb = ya_ref.shape[1], yb_ref.shape[1]
    for s in range(tm // sub):
        rows = slice(s * sub, (s + 1) * sub)
        acc = x_ref[rows, :] + jnp.dot(ya_ref[rows, :], w_ref[0:wa, :], preferred_element_type=F32)
        acc += jnp.dot(yb_ref[rows, :], w_ref[wa:wa + wb, :], preferred_element_type=F32)
        acc += jnp.dot(yc_ref[rows, :], w_ref[wa + wb:, :], preferred_element_type=F32)
        o_ref[rows, :] = acc
        h_ref[rows, :] = _rms(acc, g_ref[...]).astype(BF16)


def out_proj(x, y_gla, y_na, y_mem, w_out, gain, *, tm, sub):
    T, D = x.shape
    wa, wb, wc = y_gla.shape[1], y_na.shape[1], y_mem.shape[1]
    return pl.pallas_call(
        functools.partial(_out_proj_kernel, sub=sub),
        grid=(T // tm,),
        in_specs=[
            pl.BlockSpec((tm, D), lambda i: (i, 0)),
            pl.BlockSpec((tm, wa), lambda i: (i, 0)),
            pl.BlockSpec((tm, wb), lambda i: (i, 0)),
            pl.BlockSpec((tm, wc), lambda i: (i, 0)),
            pl.BlockSpec((wa + wb + wc, D), lambda i: (0, 0), pipeline_mode=pl.Buffered(1)),
            pl.BlockSpec((1, D), lambda i: (0, 0)),
        ],
        out_specs=[pl.BlockSpec((tm, D), lambda i: (i, 0)), pl.BlockSpec((tm, D), lambda i: (i, 0))],
        out_shape=[jax.ShapeDtypeStruct((T, D), F32), jax.ShapeDtypeStruct((T, D), BF16)],
        compiler_params=_cparams(("parallel",)),
        name="out_proj",
    )(x, y_gla, y_na, y_mem, w_out, gain)


def _ffn_kernel(x_ref, h_ref, w1_ref, w3_ref, w2_ref, o_ref):
    @pl.when(pl.program_id(1) == 0)
    def _():
        o_ref[...] = x_ref[...]

    h = h_ref[...]
    gate = jnp.dot(h, w1_ref[...], preferred_element_type=F32)
    up = jnp.dot(h, w3_ref[...], preferred_element_type=F32)
    act = (gate * jax.nn.sigmoid(gate) * up).astype(BF16)
    o_ref[...] += jnp.dot(act, w2_ref[...], preferred_element_type=F32)


def ffn(x, h, w13, w2, *, tm, tf):
    T, D = x.shape
    F = w2.shape[0]
    nf = F // tf
    return pl.pallas_call(
        _ffn_kernel,
        grid=(T // tm, nf),
        in_specs=[
            pl.BlockSpec((tm, D), lambda i, f: (i, 0)),
            pl.BlockSpec((tm, D), lambda i, f: (i, 0)),
            pl.BlockSpec((D, tf), lambda i, f: (0, f)),
            pl.BlockSpec((D, tf), lambda i, f: (0, nf + f)),
            pl.BlockSpec((tf, D), lambda i, f: (f, 0)),
        ],
        out_specs=pl.BlockSpec((tm, D), lambda i, f: (i, 0)),
        out_shape=jax.ShapeDtypeStruct((T, D), F32),
        compiler_params=_cparams(("parallel", "arbitrary")),
        name="ffn",
    )(x, h, w13, w13, w2)


IN_PROJ_TN = 1024


def kernel(x, mem, attn_norm, w_in, gla_wg2_f, gla_bg_f, gla_wg2_b, gla_bg_b, gla_out_norm,
           na_q_norm, na_k_norm, na_rpb, na_out_norm, mem_norm, mem_wkv, mem_q_norm, mem_k_norm,
           mem_out_norm, w_out, ffn_norm, ffn_w13, ffn_w2):
    B, S, D = x.shape
    depth = w_in.shape[0]
    T = B * S
    gla_dk = GLA_HEADS * GLA_HK
    gla_dv = GLA_HEADS * GLA_HV
    na_w = NA_HEADS * NA_HD
    mem_w = MEM_HEADS * MEM_HD
    gate0 = 2 * gla_dk + 2 * gla_dv
    gate1 = gate0 + 2 * GLA_RANK
    na_col0 = gate0
    mem_col0 = gate0 + 3 * na_w
    assert w_in.shape[2] - 2 * GLA_RANK == mem_col0 + mem_w and gla_dv + na_w + mem_w == w_out.shape[1]
    assert na_col0 % IN_PROJ_TN == 0 and 2 * na_w == IN_PROJ_TN

    w_in_t = jnp.swapaxes(w_in, 1, 2)
    bf = {("w_in_t", 0): w_in_t[0].astype(BF16)}
    gqk = jnp.concatenate([jnp.tile(na_q_norm * (NA_HD ** -0.5), (1, NA_HEADS)),
                           jnp.tile(na_k_norm, (1, NA_HEADS))], axis=1)
    wb = natten_bias_tables(na_rpb)

    xf = x.reshape(T, D)
    k_m, v_m = mem_kv(mem.reshape(B * mem.shape[1], D), mem_norm[:, None], mem_wkv, mem_k_norm[:, None])
    stacked = {"w_in_t": w_in_t, "w_out": w_out, "w13": ffn_w13, "w2": ffn_w2}
    for l in range(depth):
        def streams(*names):
            layers = ([0] if l == 0 else []) + ([l + 1] if l + 1 < depth else [])
            keys = [(n, j) for j in layers for n in names if (n, j) not in bf]
            return keys, [(stacked[n], j) for n, j in keys]

        keys, cast = streams("w_out", "w_in_t")
        proj, gates, y_mem, done = in_proj(
            xf, attn_norm[l][None], bf["w_in_t", l], gqk[l][None], k_m, v_m, mem_q_norm[l][None],
            mem_out_norm[l][None], layer=l, batch=B, tm=512, sub=256, tn=IN_PROJ_TN,
            qk_tile=na_col0 // IN_PROJ_TN, gate0=gate0, ngate=gate1 - gate0, cast_next=cast)
        bf.update(zip(keys, done))
        keys, cast = streams("w2")
        o_f, done = gla_direction(proj, gates, gla_wg2_f[l], gla_bg_f[l][None], batch=B, tb=1024,
                                  reverse=False, cast_next=cast)
        bf.update(zip(keys, done))
        y_gla, _ = gla_direction(proj, gates, gla_wg2_b[l], gla_bg_b[l][None], batch=B, tb=1024,
                                 reverse=True, o_fwd=o_f, out_gain=gla_out_norm[l][None])
        keys, cast = streams("w13")
        y_na, done = natten(proj, wb, na_out_norm[l][None], layer=l, batch=B, col0=na_col0, cast_next=cast)
        bf.update(zip(keys, done))
        xf, hf = out_proj(xf, y_gla, y_na, y_mem, bf["w_out", l], ffn_norm[l][None], tm=512, sub=256)
        xf = ffn(xf, hf, bf["w13", l], bf["w2", l], tm=1024, tf=512)
    return xf.reshape(B, S, D)
```

```python
import functools

import jax
import jax.numpy as jnp
from jax import lax
from jax.experimental import pallas as pl
from jax.experimental.pallas import tpu as pltpu

F32 = jnp.float32
BF16 = jnp.bfloat16

RMS_EPS = 1e-6
MASK_VALUE = -1e30

GRID_W = 64
GLA_HEADS = 4
GLA_HK = 128
GLA_HV = 256
GLA_RANK = 16
GLA_TAU = 16.0
GLA_CHUNK = 64
NA_HD = 64
NA_HEADS = 8
NA_ROWS = 8
NA_COLS = 16
MEM_HEADS = 4
MEM_HD = 128

V7X_MXU_WIDTH = 256
V7X_VMEM_BYTES = 64 * 1024 * 1024
VMEM_LIMIT_BYTES = V7X_VMEM_BYTES - 2 * 1024 * 1024

NT_DIMS = (((1,), (1,)), ((), ()))


def _cparams(sem):
    return pltpu.CompilerParams(dimension_semantics=sem, vmem_limit_bytes=VMEM_LIMIT_BYTES)


def _rms(x, gain):
    ms = jnp.mean(x * x, axis=-1, keepdims=True)
    return (x * lax.rsqrt(ms + RMS_EPS)) * gain


def _head_pair_ones(width=2 * NA_HD):
    r = lax.broadcasted_iota(jnp.int32, (width, width), 0) // NA_HD
    c = lax.broadcasted_iota(jnp.int32, (width, width), 1) // NA_HD
    return jnp.where(r == c, 1.0, 0.0).astype(BF16)


def _head_pair_rms(x, gain, ones_bd):
    ms = jnp.dot((x * x).astype(BF16), ones_bd, preferred_element_type=F32) * (1.0 / NA_HD)
    return (x * lax.rsqrt(ms + RMS_EPS)) * gain


CAST_ROW_UNIT = 16


def _cast_streams(cast_next, steps, step_of):
    in_specs, out_specs, out_shapes, operands = [], [], [], []
    for w, layer in cast_next:
        _, rows, cols = w.shape
        units = rows // CAST_ROW_UNIT
        assert rows % CAST_ROW_UNIT == 0
        mult = min(d for d in range(1, units + 1) if units % d == 0 and units // d <= steps)
        nblk = units // mult
        per = steps // nblk

        def blk(*g, per=per, nblk=nblk):
            return jnp.minimum(step_of(*g) // per, nblk - 1)

        in_specs.append(pl.BlockSpec((None, mult * CAST_ROW_UNIT, cols),
                                     lambda *g, blk=blk, layer=layer: (layer, blk(*g), 0)))
        out_specs.append(pl.BlockSpec((mult * CAST_ROW_UNIT, cols), lambda *g, blk=blk: (blk(*g), 0)))
        out_shapes.append(jax.ShapeDtypeStruct((rows, cols), BF16))
        operands.append(w)
    return in_specs, out_specs, out_shapes, operands


def _with_cast_streams(body, n_in, n_out, n_cast):
    def kern(*refs):
        ins, rest = refs[:n_in], refs[n_in:]
        cast_in, rest = rest[:n_cast], rest[n_cast:]
        outs, rest = rest[:n_out], rest[n_out:]
        cast_out, scratch = rest[:n_cast], rest[n_cast:]
        body(*ins, *outs, *scratch)
        for src, dst in zip(cast_in, cast_out):
            dst[...] = src[...].astype(BF16)
    return kern


def _mem_attention(mq, km_ref, vm_ref, gq_ref, go_ref, y_ref, rows):
    heads = range(MEM_HEADS)
    cols = [slice(h * MEM_HD, (h + 1) * MEM_HD) for h in heads]
    q = [(_rms(mq[:, c], gq_ref[...]) * (MEM_HD ** -0.5)).astype(BF16) for c in cols]
    s = [lax.dot_general(q[h], km_ref[:, cols[h]], NT_DIMS, preferred_element_type=F32) for h in heads]
    p = [jnp.exp(sh - jnp.max(sh, axis=-1, keepdims=True)) for sh in s]
    l = [jnp.sum(ph, axis=-1, keepdims=True) for ph in p]
    o = [jnp.dot(p[h].astype(BF16), vm_ref[:, cols[h]], preferred_element_type=F32) / l[h] for h in heads]
    for h in heads:
        y_ref[rows, cols[h]] = _rms(o[h], go_ref[:, cols[h]]).astype(y_ref.dtype)


def _in_proj_kernel(x_ref, g_ref, wt_ref, gqk_ref, km_ref, vm_ref, gmq_ref, gmo_ref,
                    proj_ref, gate_ref, ymem_ref, *, sub, tn, qk_tile, gate0):
    tm = x_ref.shape[0]
    ngate = gate_ref.shape[1]
    lw = V7X_MXU_WIDTH
    ones_bd = _head_pair_ones(lw)
    n_proj = proj_ref.shape[1]
    ntile = (wt_ref.shape[0] - ngate) // tn
    for s in range(tm // sub):
        rows = slice(s * sub, (s + 1) * sub)
        xn = _rms(x_ref[rows, :], g_ref[...]).astype(BF16)
        gate_ref[rows, :] = lax.dot_general(xn, wt_ref[gate0:gate0 + ngate, :], NT_DIMS,
                                            preferred_element_type=F32)
        for n in [ntile - 1] + list(range(ntile - 1)):
            w0 = n * tn + (ngate if n * tn >= gate0 else 0)
            acc = lax.dot_general(xn, wt_ref[w0:w0 + tn, :], NT_DIMS, preferred_element_type=F32)
            keep = min(tn, n_proj - n * tn)
            if n == qk_tile:
                for c in range(tn // lw):
                    cols = slice(c * lw, (c + 1) * lw)
                    proj_ref[rows, n * tn + c * lw:n * tn + (c + 1) * lw] = _head_pair_rms(
                        acc[:, cols], gqk_ref[:, cols], ones_bd).astype(BF16)
            else:
                proj_ref[rows, n * tn:n * tn + keep] = acc[:, :keep].astype(BF16)
            if keep < tn:
                _mem_attention(acc[:, keep:], km_ref, vm_ref, gmq_ref, gmo_ref, ymem_ref, rows)


def in_proj(x, gain, w_t, gqk, k_m, v_m, gmq, gmo, *, layer, batch, tm, sub, tn, qk_tile, gate0, ngate,
            cast_next=()):
    T, D = x.shape
    n_in = w_t.shape[0]
    N = n_in - ngate
    G = ngate
    assert gate0 % tn == 0 and N % tn == 0
    mem_w = MEM_HEADS * MEM_HD
    n_mem = k_m.shape[1] // batch
    per_b = (T // batch) // tm
    args = [x, gain, w_t, gqk, k_m, v_m, gmq, gmo]
    c_in, c_out, c_shapes, c_args = _cast_streams(cast_next, T // tm, lambda i: i)
    kern = functools.partial(_in_proj_kernel, sub=sub, tn=tn, qk_tile=qk_tile, gate0=gate0)
    outs = pl.pallas_call(
        _with_cast_streams(kern, len(args), 3, len(c_args)),
        grid=(T // tm,),
        in_specs=[
            pl.BlockSpec((tm, D), lambda i: (i, 0)),
            pl.BlockSpec((1, D), lambda i: (0, 0)),
            pl.BlockSpec((n_in, D), lambda i: (0, 0), pipeline_mode=pl.Buffered(1)),
            pl.BlockSpec((1, tn), lambda i: (0, 0)),
            pl.BlockSpec((None, n_mem, mem_w), lambda i: (layer, i // per_b, 0)),
            pl.BlockSpec((None, n_mem, mem_w), lambda i: (layer, i // per_b, 0)),
            pl.BlockSpec((1, MEM_HD), lambda i: (0, 0)),
            pl.BlockSpec((1, mem_w), lambda i: (0, 0)),
        ] + c_in,
        out_specs=[
            pl.BlockSpec((tm, N - mem_w), lambda i: (i, 0)),
            pl.BlockSpec((tm, G), lambda i: (i, 0)),
            pl.BlockSpec((tm, mem_w), lambda i: (i, 0)),
        ] + c_out,
        out_shape=[jax.ShapeDtypeStruct((T, N - mem_w), BF16), jax.ShapeDtypeStruct((T, G), F32),
                   jax.ShapeDtypeStruct((T, mem_w), BF16)] + c_shapes,
        compiler_params=_cparams(("arbitrary",)),
        name="in_proj",
    )(*args, *c_args)
    return outs[0], outs[1], outs[2], outs[3:]


def _log_sigmoid(z):
    return jnp.minimum(z, 0.0) - jnp.log(1.0 + jnp.exp(-jnp.abs(z)))


def _gla_kernel(q_ref, k_ref, v_ref, gate_ref, wg_ref, bg_ref, *rest, reverse, final, gate_off):
    if final:
        r_ref, of_ref, gn_ref, o_ref, st_ref = rest
    else:
        o_ref, st_ref = rest
    C = GLA_CHUNK
    H, dk, dv = GLA_HEADS, GLA_HK, GLA_HV
    tb = q_ref.shape[0]
    nchunk = tb // C

    @pl.when(pl.program_id(1) == 0)
    def _():
        st_ref[...] = jnp.zeros_like(st_ref)

    g = gate_ref[:, gate_off:gate_off + GLA_RANK]
    z = jnp.dot(g.astype(BF16), wg_ref[...].astype(BF16), preferred_element_type=F32) + bg_ref[...]
    la = _log_sigmoid(z) * (1.0 / GLA_TAU)

    la_t = jnp.concatenate([la[c * C:(c + 1) * C, :] for c in range(nchunk)], axis=1)
    row = lax.broadcasted_iota(jnp.int32, (C, C), 0)
    col = lax.broadcasted_iota(jnp.int32, (C, C), 1)
    if reverse:
        tri = (col >= row)
        keep = col > row
    else:
        tri = (col <= row)
        keep = col <= row
    tri = jnp.where(tri, 1.0, 0.0).astype(BF16)
    la_hi = la_t.astype(BF16)
    la_lo = (la_t - la_hi.astype(F32)).astype(BF16)
    cum = (jnp.dot(tri, la_hi, preferred_element_type=F32)
           + jnp.dot(tri, la_lo, preferred_element_type=F32))

    order = range(nchunk - 1, -1, -1) if reverse else range(nchunk)
    heads = range(H)
    kcols = [slice(h * dk, (h + 1) * dk) for h in heads]
    vcols = [slice(h * dv, (h + 1) * dv) for h in heads]
    for c in order:
        rows = slice(c * C, (c + 1) * C)
        cum_c = cum[:, c * H * dk:(c + 1) * H * dk]
        last = cum_c[0:1, :] if reverse else cum_c[C - 1:C, :]
        qc = q_ref[rows, :].astype(F32) * (GLA_HK ** -0.5)
        kc = k_ref[rows, :].astype(F32)
        q_e = (qc * jnp.exp(cum_c)).astype(BF16)
        k_e = (kc * jnp.exp(-cum_c)).astype(BF16)
        k_end = (kc * jnp.exp(last - cum_c)).astype(BF16)
        decay = jnp.exp(last)
        vc = [v_ref[rows, vcols[h]] for h in heads]
        sc = [lax.dot_general(q_e[:, kcols[h]], k_e[:, kcols[h]], NT_DIMS, preferred_element_type=F32)
              for h in heads]
        sc = [jnp.where(keep, s, 0.0).astype(BF16) for s in sc]
        st = [st_ref[h] for h in heads]
        o = [jnp.dot(sc[h], vc[h], preferred_element_type=F32)
             + lax.dot_general(q_e[:, kcols[h]], st[h].astype(BF16), NT_DIMS, preferred_element_type=F32)
             for h in heads]
        for h in heads:
            st_ref[h] = (st[h] * decay[:, kcols[h]]
                         + jnp.dot(vc[h].T, k_end[:, kcols[h]], preferred_element_type=F32))
        for h in heads:
            if final:
                oh = o[h] + of_ref[rows, vcols[h]].astype(F32)
                y = _rms(oh, gn_ref[:, vcols[h]])
                r = r_ref[rows, vcols[h]].astype(F32)
                o_ref[rows, vcols[h]] = (y * (r * jax.nn.sigmoid(r))).astype(o_ref.dtype)
            else:
                o_ref[rows, vcols[h]] = o[h].astype(o_ref.dtype)


def gla_direction(proj, gates, wg2, bg, *, batch, tb, reverse, o_fwd=None, out_gain=None, cast_next=()):
    T = proj.shape[0]
    S = T // batch
    nb = S // tb
    final = o_fwd is not None
    H, dk, dv = GLA_HEADS, GLA_HK, GLA_HV
    wk, wv = H * dk, H * dv

    def tok(b, i):
        return b * nb + ((nb - 1 - i) if reverse else i)

    in_specs = [
        pl.BlockSpec((tb, wk), lambda b, i: (tok(b, i), 0)),
        pl.BlockSpec((tb, wk), lambda b, i: (tok(b, i), 1)),
        pl.BlockSpec((tb, wv), lambda b, i: (tok(b, i), (2 * wk) // wv)),
        pl.BlockSpec((tb, gates.shape[1]), lambda b, i: (tok(b, i), 0)),
        pl.BlockSpec((GLA_RANK, wk), lambda b, i: (0, 0)),
        pl.BlockSpec((1, wk), lambda b, i: (0, 0)),
    ]
    args = [proj, proj, proj, gates, wg2, bg]
    if final:
        in_specs += [
            pl.BlockSpec((tb, wv), lambda b, i: (tok(b, i), (2 * wk) // wv + 1)),
            pl.BlockSpec((tb, wv), lambda b, i: (tok(b, i), 0)),
            pl.BlockSpec((1, wv), lambda b, i: (0, 0)),
        ]
        args += [proj, o_fwd, out_gain]
    kern = functools.partial(_gla_kernel, reverse=reverse, final=final,
                             gate_off=GLA_RANK if reverse else 0)
    c_in, c_out, c_shapes, c_args = _cast_streams(cast_next, batch * nb, lambda b, i: b * nb + i)
    outs = pl.pallas_call(
        _with_cast_streams(kern, len(args), 1, len(c_args)),
        grid=(batch, nb),
        in_specs=in_specs + c_in,
        out_specs=[pl.BlockSpec((tb, wv), lambda b, i: (tok(b, i), 0))] + c_out,
        out_shape=[jax.ShapeDtypeStruct((T, wv), BF16)] + c_shapes,
        scratch_shapes=[pltpu.VMEM((H, dv, dk), F32)],
        compiler_params=_cparams(("arbitrary", "arbitrary")),
        name="gla_bwd" if reverse else "gla_fwd",
    )(*args, *c_args)
    return outs[0], outs[1:]


NA_QROWS = 8
NA_DR = 2 * NA_ROWS - 1
NA_DC = 2 * NA_COLS - 1


def _natten_bias_kernel(rpb_ref, o_ref):
    W = GRID_W
    qc = lax.broadcasted_iota(jnp.int32, (W, 2 * W), 0)
    ln = lax.broadcasted_iota(jnp.int32, (W, 2 * W), 1)
    kc = ln & (W - 1)
    dc_idx = jnp.clip(kc - qc, -(NA_COLS - 1), NA_COLS - 1) + (NA_COLS - 1)
    cs = jnp.clip(qc - NA_COLS // 2, 0, W - NA_COLS)
    in_win = (kc >= cs) & (kc < cs + NA_COLS)
    left = ln < W
    for h in range(o_ref.shape[0]):
        toeplitz = []
        for dr in range(NA_DR):
            row = jnp.broadcast_to(rpb_ref[h, dr:dr + 1, :], (W, 2 * W))
            t = jnp.take_along_axis(row, dc_idx, axis=1)
            toeplitz.append(jnp.where(in_win, t, MASK_VALUE))
        for dr0 in range(NA_ROWS):
            for jp in range(NA_ROWS // 2):
                o_ref[h, dr0, :, jp * 2 * W:(jp + 1) * 2 * W] = jnp.where(
                    left, toeplitz[dr0 + 2 * jp], toeplitz[dr0 + 2 * jp + 1])


def natten_bias_tables(rpb):
    L, H = rpb.shape[:2]
    W = GRID_W
    rpb = jnp.pad(rpb, ((0, 0), (0, 0), (0, 0), (0, 2 * W - NA_DC)))
    return pl.pallas_call(
        _natten_bias_kernel,
        grid=(L,),
        in_specs=[pl.BlockSpec((None, H, NA_DR, 2 * W), lambda l: (l, 0, 0, 0))],
        out_specs=pl.BlockSpec((None, H, NA_ROWS, W, NA_ROWS * W), lambda l: (l, 0, 0, 0, 0)),
        out_shape=jax.ShapeDtypeStruct((L, H, NA_ROWS, W, NA_ROWS * W), F32),
        compiler_params=_cparams(("parallel",)),
        name="natten_bias",
    )(rpb)


def _natten_kernel(q_ref, kp_ref, kc_ref, kn_ref, vp_ref, vc_ref, vn_ref, wb_ref, go_ref, o_ref,
                   o_scr, *, rows):
    W = GRID_W
    nq = q_ref.shape[0]
    lw = 2 * NA_HD
    nwin = NA_ROWS * W

    def window(refs, start, cols):
        parts = []
        for b, ref in enumerate(refs):
            lo, hi = max(start, b * nq), min(start + nwin, (b + 1) * nq)
            if lo < hi:
                parts.append(ref[lo - b * nq:hi - b * nq, cols])
        return parts[0] if len(parts) == 1 else jnp.concatenate(parts, axis=0)

    k_refs = (kp_ref, kc_ref, kn_ref)
    v_refs = (vp_ref, vc_ref, vn_ref)
    head0 = lax.broadcasted_iota(jnp.int32, (1, lw), 1) < NA_HD
    blk = pl.program_id(1)
    nblk = rows // NA_QROWS

    def step(r0):
        for a in range(NA_QROWS):
            rs = min(max(r0 + a - NA_ROWS // 2, 0), rows - NA_ROWS)
            off = rs - (r0 - NA_QROWS)
            dr0 = rs - (r0 + a) + (NA_ROWS - 1)
            qrows = slice(a * W, (a + 1) * W)
            pairs = range(NA_HEADS // 2)
            cols = [slice(p * lw, (p + 1) * lw) for p in pairs]
            s = []
            for p in pairs:
                qa = q_ref[qrows, cols[p]]
                zero = jnp.zeros_like(qa)
                lhs = jnp.concatenate([jnp.where(head0, qa, zero), jnp.where(head0, zero, qa)], axis=0)
                sp = lax.dot_general(lhs, window(k_refs, off * W, cols[p]), NT_DIMS,
                                     preferred_element_type=F32)
                s.append(sp + jnp.concatenate([wb_ref[2 * p, dr0], wb_ref[2 * p + 1, dr0]], axis=0))
            m = [jnp.max(sp, axis=-1, keepdims=True) for sp in s]
            e = [jnp.exp(sp - mp) for sp, mp in zip(s, m)]
            l = [jnp.sum(ep, axis=-1, keepdims=True) for ep in e]
            r = [jnp.dot(e[p].astype(BF16), window(v_refs, off * W, cols[p]), preferred_element_type=F32)
                 for p in pairs]
            for p in pairs:
                rp = r[p] / l[p]
                o_scr[qrows, cols[p]] = jnp.where(head0, rp[0:W], rp[W:2 * W])

    pl.when(blk == 0)(lambda: step(0))
    pl.when(jnp.logical_and(blk > 0, blk < nblk - 1))(lambda: step(NA_QROWS))
    pl.when(blk == nblk - 1)(lambda: step(rows - NA_QROWS))

    gw = V7X_MXU_WIDTH
    ones_bd = _head_pair_ones(gw)
    for g in range(o_ref.shape[1] // gw):
        cols = slice(g * gw, (g + 1) * gw)
        o_ref[:, cols] = _head_pair_rms(o_scr[:, cols], go_ref[:, cols], ones_bd).astype(o_ref.dtype)


def natten(proj, wb, go, *, layer, batch, col0, cast_next=()):
    T = proj.shape[0]
    S = T // batch
    rows = S // GRID_W
    nq = NA_QROWS * GRID_W
    nblk = S // nq
    width = NA_HEADS * NA_HD
    cq = col0 // width

    def prev(i):
        return jnp.maximum(i - 1, 0)

    def nxt(i):
        return jnp.minimum(i + 1, nblk - 1)

    in_specs = [
        pl.BlockSpec((nq, width), lambda b, i: (b * nblk + i, cq)),
        pl.BlockSpec((nq, width), lambda b, i: (b * nblk + prev(i), cq + 1)),
        pl.BlockSpec((nq, width), lambda b, i: (b * nblk + i, cq + 1)),
        pl.BlockSpec((nq, width), lambda b, i: (b * nblk + nxt(i), cq + 1)),
        pl.BlockSpec((nq, width), lambda b, i: (b * nblk + prev(i), cq + 2)),
        pl.BlockSpec((nq, width), lambda b, i: (b * nblk + i, cq + 2)),
        pl.BlockSpec((nq, width), lambda b, i: (b * nblk + nxt(i), cq + 2)),
        pl.BlockSpec((None,) + wb.shape[1:], lambda b, i: (layer, 0, 0, 0, 0), pipeline_mode=pl.Buffered(1)),
        pl.BlockSpec((1, width), lambda b, i: (0, 0)),
    ]
    args = [proj, proj, proj, proj, proj, proj, proj, wb, go]
    c_in, c_out, c_shapes, c_args = _cast_streams(cast_next, batch * nblk, lambda b, i: b * nblk + i)
    outs = pl.pallas_call(
        _with_cast_streams(functools.partial(_natten_kernel, rows=rows), len(args), 1, len(c_args)),
        grid=(batch, nblk),
        in_specs=in_specs + c_in,
        out_specs=[pl.BlockSpec((nq, width), lambda b, i: (b * nblk + i, 0))] + c_out,
        out_shape=[jax.ShapeDtypeStruct((T, width), BF16)] + c_shapes,
        scratch_shapes=[pltpu.VMEM((nq, width), F32)],
        compiler_params=_cparams(("arbitrary", "arbitrary")),
        name="natten",
    )(*args, *c_args)
    return outs[0], outs[1:]


def _mem_kv_kernel(mem_ref, g_ref, w_ref, gk_ref, k_ref, v_ref):
    mn = _rms(mem_ref[...], g_ref[...]).astype(BF16)
    kv = jnp.dot(mn, w_ref[...].astype(BF16), preferred_element_type=F32)
    width = k_ref.shape[1]
    for h in range(MEM_HEADS):
        cols = slice(h * MEM_HD, (h + 1) * MEM_HD)
        k_ref[:, cols] = _rms(kv[:, cols], gk_ref[...]).astype(BF16)
    v_ref[...] = kv[:, width:].astype(BF16)


def mem_kv(mem2d, gain, wkv, gk):
    n, D = mem2d.shape
    L = wkv.shape[0]
    width = wkv.shape[2] // 2
    return pl.pallas_call(
        _mem_kv_kernel,
        grid=(L,),
        in_specs=[
            pl.BlockSpec((n, D), lambda l: (0, 0)),
            pl.BlockSpec((None, 1, D), lambda l: (l, 0, 0)),
            pl.BlockSpec((None, D, 2 * width), lambda l: (l, 0, 0)),
            pl.BlockSpec((None, 1, MEM_HD), lambda l: (l, 0, 0)),
        ],
        out_specs=[pl.BlockSpec((None, n, width), lambda l: (l, 0, 0)),
                   pl.BlockSpec((None, n, width), lambda l: (l, 0, 0))],
        out_shape=[jax.ShapeDtypeStruct((L, n, width), BF16), jax.ShapeDtypeStruct((L, n, width), BF16)],
        compiler_params=_cparams(("parallel",)),
        name="mem_kv",
    )(mem2d, gain, wkv, gk)


def _out_proj_kernel(x_ref, ya_ref, yb_ref, yc_ref, w_ref, g_ref, o_ref, h_ref, *, sub):
    tm = x_ref.shape[0]
    wa, wb = ya_ref.shape[1], yb_ref.shape[1]
    for s in range(tm // sub):
        rows = slice(s * sub, (s + 1) * sub)
        acc = x_ref[rows, :] + jnp.dot(ya_ref[rows, :], w_ref[0:wa, :], preferred_element_type=F32)
        acc += jnp.dot(yb_ref[rows, :], w_ref[wa:wa + wb, :], preferred_element_type=F32)
        acc += jnp.dot(yc_ref[rows, :], w_ref[wa + wb:, :], preferred_element_type=F32)
        o_ref[rows, :] = acc
        h_ref[rows, :] = _rms(acc, g_ref[...]).astype(BF16)


def out_proj(x, y_gla, y_na, y_mem, w_out, gain, *, tm, sub):
    T, D = x.shape
    wa, wb, wc = y_gla.shape[1], y_na.shape[1], y_mem.shape[1]
    return pl.pallas_call(
        functools.partial(_out_proj_kernel, sub=sub),
        grid=(T // tm,),
        in_specs=[
            pl.BlockSpec((tm, D), lambda i: (i, 0)),
            pl.BlockSpec((tm, wa), lambda i: (i, 0)),
            pl.BlockSpec((tm, wb), lambda i: (i, 0)),
            pl.BlockSpec((tm, wc), lambda i: (i, 0)),
            pl.BlockSpec((wa + wb + wc, D), lambda i: (0, 0), pipeline_mode=pl.Buffered(1)),
            pl.BlockSpec((1, D), lambda i: (0, 0)),
        ],
        out_specs=[pl.BlockSpec((tm, D), lambda i: (i, 0)), pl.BlockSpec((tm, D), lambda i: (i, 0))],
        out_shape=[jax.ShapeDtypeStruct((T, D), F32), jax.ShapeDtypeStruct((T, D), BF16)],
        compiler_params=_cparams(("parallel",)),
        name="out_proj",
    )(x, y_gla, y_na, y_mem, w_out, gain)


def _ffn_kernel(x_ref, h_ref, w1_ref, w3_ref, w2_ref, o_ref):
    def step(acc_ref):
        h = h_ref[...]
        gate = jnp.dot(h, w1_ref[...], preferred_element_type=F32)
        up = jnp.dot(h, w3_ref[...], preferred_element_type=F32)
        act = (gate * jax.nn.sigmoid(gate) * up).astype(BF16)
        o_ref[...] = acc_ref[...] + jnp.dot(act, w2_ref[...], preferred_element_type=F32)

    pl.when(pl.program_id(1) == 0)(lambda: step(x_ref))
    pl.when(pl.program_id(1) != 0)(lambda: step(o_ref))


def ffn(x, h, w13, w2, *, tm, tf):
    T, D = x.shape
    F = w2.shape[0]
    nf = F // tf
    return pl.pallas_call(
        _ffn_kernel,
        grid=(T // tm, nf),
        in_specs=[
            pl.BlockSpec((tm, D), lambda i, f: (i, 0)),
            pl.BlockSpec((tm, D), lambda i, f: (i, 0)),
            pl.BlockSpec((D, tf), lambda i, f: (0, f)),
            pl.BlockSpec((D, tf), lambda i, f: (0, nf + f)),
            pl.BlockSpec((tf, D), lambda i, f: (f, 0)),
        ],
        out_specs=pl.BlockSpec((tm, D), lambda i, f: (i, 0)),
        out_shape=jax.ShapeDtypeStruct((T, D), F32),
        compiler_params=_cparams(("parallel", "arbitrary")),
        name="ffn",
    )(x, h, w13, w13, w2)


IN_PROJ_TN = 1024


def kernel(x, mem, attn_norm, w_in, gla_wg2_f, gla_bg_f, gla_wg2_b, gla_bg_b, gla_out_norm,
           na_q_norm, na_k_norm, na_rpb, na_out_norm, mem_norm, mem_wkv, mem_q_norm, mem_k_norm,
           mem_out_norm, w_out, ffn_norm, ffn_w13, ffn_w2):
    B, S, D = x.shape
    depth = w_in.shape[0]
    T = B * S
    gla_dk = GLA_HEADS * GLA_HK
    gla_dv = GLA_HEADS * GLA_HV
    na_w = NA_HEADS * NA_HD
    mem_w = MEM_HEADS * MEM_HD
    gate0 = 2 * gla_dk + 2 * gla_dv
    gate1 = gate0 + 2 * GLA_RANK
    na_col0 = gate0
    mem_col0 = gate0 + 3 * na_w
    assert w_in.shape[2] - 2 * GLA_RANK == mem_col0 + mem_w and gla_dv + na_w + mem_w == w_out.shape[1]
    assert na_col0 % IN_PROJ_TN == 0 and 2 * na_w == IN_PROJ_TN

    w_in_t = jnp.swapaxes(w_in, 1, 2)
    bf = {("w_in_t", 0): w_in_t[0].astype(BF16)}
    gqk = jnp.concatenate([jnp.tile(na_q_norm * (NA_HD ** -0.5), (1, NA_HEADS)),
                           jnp.tile(na_k_norm, (1, NA_HEADS))], axis=1)
    wb = natten_bias_tables(na_rpb)

    xf = x.reshape(T, D)
    k_m, v_m = mem_kv(mem.reshape(B * mem.shape[1], D), mem_norm[:, None], mem_wkv, mem_k_norm[:, None])
    stacked = {"w_in_t": w_in_t, "w_out": w_out, "w13": ffn_w13, "w2": ffn_w2}
    for l in range(depth):
        def streams(*names):
            layers = ([0] if l == 0 else []) + ([l + 1] if l + 1 < depth else [])
            keys = [(n, j) for j in layers for n in names if (n, j) not in bf]
            return keys, [(stacked[n], j) for n, j in keys]

        keys, cast = streams("w_out", "w_in_t")
        proj, gates, y_mem, done = in_proj(
            xf, attn_norm[l][None], bf["w_in_t", l], gqk[l][None], k_m, v_m, mem_q_norm[l][None],
            mem_out_norm[l][None], layer=l, batch=B, tm=512, sub=256, tn=IN_PROJ_TN,
            qk_tile=na_col0 // IN_PROJ_TN, gate0=gate0, ngate=gate1 - gate0, cast_next=cast)
        bf.update(zip(keys, done))
        keys, cast = streams("w2")
        o_f, done = gla_direction(proj, gates, gla_wg2_f[l], gla_bg_f[l][None], batch=B, tb=1024,
                                  reverse=False, cast_next=cast)
        bf.update(zip(keys, done))
        y_gla, _ = gla_direction(proj, gates, gla_wg2_b[l], gla_bg_b[l][None], batch=B, tb=1024,
                                 reverse=True, o_fwd=o_f, out_gain=gla_out_norm[l][None])
        keys, cast = streams("w13")
        y_na, done = natten(proj, wb, na_out_norm[l][None], layer=l, batch=B, col0=na_col0, cast_next=cast)
        bf.update(zip(keys, done))
        xf, hf = out_proj(xf, y_gla, y_na, y_mem, bf["w_out", l], ffn_norm[l][None], tm=512, sub=256)
        xf = ffn(xf, hf, bf["w13", l], bf["w2", l], tm=1024, tf=512)
    return xf.reshape(B, S, D)
```

```python
import functools

import jax
import jax.numpy as jnp
from jax import lax
from jax.experimental import pallas as pl
from jax.experimental.pallas import tpu as pltpu

F32 = jnp.float32
BF16 = jnp.bfloat16

RMS_EPS = 1e-6
MASK_VALUE = -1e30

GRID_W = 64
GLA_HEADS = 4
GLA_HK = 128
GLA_HV = 256
GLA_RANK = 16
GLA_TAU = 16.0
GLA_CHUNK = 64
NA_HD = 64
NA_HEADS = 8
NA_ROWS = 8
NA_COLS = 16
MEM_HEADS = 4
MEM_HD = 128

V7X_MXU_WIDTH = 256
V7X_VMEM_BYTES = 64 * 1024 * 1024
VMEM_LIMIT_BYTES = V7X_VMEM_BYTES - 2 * 1024 * 1024

NT_DIMS = (((1,), (1,)), ((), ()))


def _cparams(sem):
    return pltpu.CompilerParams(dimension_semantics=sem, vmem_limit_bytes=VMEM_LIMIT_BYTES)


def _rms(x, gain):
    ms = jnp.mean(x * x, axis=-1, keepdims=True)
    return (x * lax.rsqrt(ms + RMS_EPS)) * gain


def _head_pair_ones(width=2 * NA_HD):
    r = lax.broadcasted_iota(jnp.int32, (width, width), 0) // NA_HD
    c = lax.broadcasted_iota(jnp.int32, (width, width), 1) // NA_HD
    return jnp.where(r == c, 1.0, 0.0).astype(BF16)


def _head_pair_rms(x, gain, ones_bd):
    ms = jnp.dot((x * x).astype(BF16), ones_bd, preferred_element_type=F32) * (1.0 / NA_HD)
    return (x * lax.rsqrt(ms + RMS_EPS)) * gain


CAST_ROW_UNIT = 16


def _cast_streams(cast_next, steps, step_of):
    in_specs, out_specs, out_shapes, operands = [], [], [], []
    for w, layer in cast_next:
        _, rows, cols = w.shape
        units = rows // CAST_ROW_UNIT
        assert rows % CAST_ROW_UNIT == 0
        mult = min(d for d in range(1, units + 1) if units % d == 0 and units // d <= steps)
        nblk = units // mult
        per = steps // nblk

        def blk(*g, per=per, nblk=nblk):
            return jnp.minimum(step_of(*g) // per, nblk - 1)

        in_specs.append(pl.BlockSpec((None, mult * CAST_ROW_UNIT, cols),
                                     lambda *g, blk=blk, layer=layer: (layer, blk(*g), 0)))
        out_specs.append(pl.BlockSpec((mult * CAST_ROW_UNIT, cols), lambda *g, blk=blk: (blk(*g), 0)))
        out_shapes.append(jax.ShapeDtypeStruct((rows, cols), BF16))
        operands.append(w)
    return in_specs, out_specs, out_shapes, operands


def _with_cast_streams(body, n_in, n_out, n_cast):
    def kern(*refs):
        ins, rest = refs[:n_in], refs[n_in:]
        cast_in, rest = rest[:n_cast], rest[n_cast:]
        outs, rest = rest[:n_out], rest[n_out:]
        cast_out, scratch = rest[:n_cast], rest[n_cast:]
        body(*ins, *outs, *scratch)
        for src, dst in zip(cast_in, cast_out):
            dst[...] = src[...].astype(BF16)
    return kern


def _mem_attention(mq, km_ref, vm_ref, gq_ref, go_ref, y_ref, rows):
    heads = range(MEM_HEADS)
    cols = [slice(h * MEM_HD, (h + 1) * MEM_HD) for h in heads]
    q = [(_rms(mq[:, c], gq_ref[...]) * (MEM_HD ** -0.5)).astype(BF16) for c in cols]
    s = [lax.dot_general(q[h], km_ref[:, cols[h]], NT_DIMS, preferred_element_type=F32) for h in heads]
    p = [jnp.exp(sh - jnp.max(sh, axis=-1, keepdims=True)) for sh in s]
    l = [jnp.sum(ph, axis=-1, keepdims=True) for ph in p]
    o = [jnp.dot(p[h].astype(BF16), vm_ref[:, cols[h]], preferred_element_type=F32) / l[h] for h in heads]
    for h in heads:
        y_ref[rows, cols[h]] = _rms(o[h], go_ref[:, cols[h]]).astype(y_ref.dtype)


def _in_proj_kernel(x_ref, g_ref, wt_ref, gqk_ref, km_ref, vm_ref, gmq_ref, gmo_ref, xnext_ref,
                    proj_ref, gate_ref, ymem_ref, xn0_ref, *, sub, tn, qk_tile, gate0):
    tm = x_ref.shape[0]
    ngate = gate_ref.shape[1]
    lw = V7X_MXU_WIDTH
    ones_bd = _head_pair_ones(lw)
    n_proj = proj_ref.shape[1]
    ntile = (wt_ref.shape[0] - ngate) // tn

    @pl.when(pl.program_id(0) == 0)
    def _():
        xn0_ref[...] = _rms(x_ref[0:sub, :], g_ref[...]).astype(BF16)

    for s in range(tm // sub):
        rows = slice(s * sub, (s + 1) * sub)
        xn = xn0_ref[...] if s == 0 else _rms(x_ref[rows, :], g_ref[...]).astype(BF16)
        gate_ref[rows, :] = lax.dot_general(xn, wt_ref[gate0:gate0 + ngate, :], NT_DIMS,
                                            preferred_element_type=F32)
        for n in [ntile - 1] + list(range(ntile - 1)):
            w0 = n * tn + (ngate if n * tn >= gate0 else 0)
            acc = lax.dot_general(xn, wt_ref[w0:w0 + tn, :], NT_DIMS, preferred_element_type=F32)
            keep = min(tn, n_proj - n * tn)
            if n == qk_tile:
                for c in range(tn // lw):
                    cols = slice(c * lw, (c + 1) * lw)
                    proj_ref[rows, n * tn + c * lw:n * tn + (c + 1) * lw] = _head_pair_rms(
                        acc[:, cols], gqk_ref[:, cols], ones_bd).astype(BF16)
            else:
                proj_ref[rows, n * tn:n * tn + keep] = acc[:, :keep].astype(BF16)
            if keep < tn:
                _mem_attention(acc[:, keep:], km_ref, vm_ref, gmq_ref, gmo_ref, ymem_ref, rows)
    xn0_ref[...] = _rms(xnext_ref[...], g_ref[...]).astype(BF16)


def in_proj(x, gain, w_t, gqk, k_m, v_m, gmq, gmo, *, layer, batch, tm, sub, tn, qk_tile, gate0, ngate,
            cast_next=()):
    T, D = x.shape
    n_in = w_t.shape[0]
    N = n_in - ngate
    G = ngate
    assert gate0 % tn == 0 and N % tn == 0
    mem_w = MEM_HEADS * MEM_HD
    n_mem = k_m.shape[1] // batch
    per_b = (T // batch) // tm
    args = [x, gain, w_t, gqk, k_m, v_m, gmq, gmo, x]
    nstep = T // tm
    c_in, c_out, c_shapes, c_args = _cast_streams(cast_next, T // tm, lambda i: i)
    kern = functools.partial(_in_proj_kernel, sub=sub, tn=tn, qk_tile=qk_tile, gate0=gate0)
    outs = pl.pallas_call(
        _with_cast_streams(kern, len(args), 3, len(c_args)),
        grid=(T // tm,),
        in_specs=[
            pl.BlockSpec((tm, D), lambda i: (i, 0)),
            pl.BlockSpec((1, D), lambda i: (0, 0)),
            pl.BlockSpec((n_in, D), lambda i: (0, 0), pipeline_mode=pl.Buffered(1)),
            pl.BlockSpec((1, tn), lambda i: (0, 0)),
            pl.BlockSpec((None, n_mem, mem_w), lambda i: (layer, i // per_b, 0)),
            pl.BlockSpec((None, n_mem, mem_w), lambda i: (layer, i // per_b, 0)),
            pl.BlockSpec((1, MEM_HD), lambda i: (0, 0)),
            pl.BlockSpec((1, mem_w), lambda i: (0, 0)),
            pl.BlockSpec((sub, D), lambda i: (jnp.minimum(i + 1, nstep - 1) * (tm // sub), 0)),
        ] + c_in,
        out_specs=[
            pl.BlockSpec((tm, N - mem_w), lambda i: (i, 0)),
            pl.BlockSpec((tm, G), lambda i: (i, 0)),
            pl.BlockSpec((tm, mem_w), lambda i: (i, 0)),
        ] + c_out,
        out_shape=[jax.ShapeDtypeStruct((T, N - mem_w), BF16), jax.ShapeDtypeStruct((T, G), F32),
                   jax.ShapeDtypeStruct((T, mem_w), BF16)] + c_shapes,
        scratch_shapes=[pltpu.VMEM((sub, D), BF16)],
        compiler_params=_cparams(("arbitrary",)),
        name="in_proj",
    )(*args, *c_args)
    return outs[0], outs[1], outs[2], outs[3:]


def _log_sigmoid(z):
    return jnp.minimum(z, 0.0) - jnp.log(1.0 + jnp.exp(-jnp.abs(z)))


def _gla_kernel(q_ref, k_ref, v_ref, gate_ref, wg_ref, bg_ref, *rest, reverse, final, gate_off):
    if final:
        r_ref, of_ref, gn_ref, o_ref, st_ref = rest
    else:
        o_ref, st_ref = rest
    C = GLA_CHUNK
    H, dk, dv = GLA_HEADS, GLA_HK, GLA_HV
    tb = q_ref.shape[0]
    nchunk = tb // C

    @pl.when(pl.program_id(1) == 0)
    def _():
        st_ref[...] = jnp.zeros_like(st_ref)

    g = gate_ref[:, gate_off:gate_off + GLA_RANK]
    z = jnp.dot(g.astype(BF16), wg_ref[...].astype(BF16), preferred_element_type=F32) + bg_ref[...]
    la = _log_sigmoid(z) * (1.0 / GLA_TAU)

    la_t = jnp.concatenate([la[c * C:(c + 1) * C, :] for c in range(nchunk)], axis=1)
    row = lax.broadcasted_iota(jnp.int32, (C, C), 0)
    col = lax.broadcasted_iota(jnp.int32, (C, C), 1)
    if reverse:
        tri = (col >= row)
        keep = col > row
    else:
        tri = (col <= row)
        keep = col <= row
    tri = jnp.where(tri, 1.0, 0.0).astype(BF16)
    la_hi = la_t.astype(BF16)
    la_lo = (la_t - la_hi.astype(F32)).astype(BF16)
    cum = (jnp.dot(tri, la_hi, preferred_element_type=F32)
           + jnp.dot(tri, la_lo, preferred_element_type=F32))

    order = range(nchunk - 1, -1, -1) if reverse else range(nchunk)
    heads = range(H)
    kcols = [slice(h * dk, (h + 1) * dk) for h in heads]
    vcols = [slice(h * dv, (h + 1) * dv) for h in heads]
    for c in order:
        rows = slice(c * C, (c + 1) * C)
        cum_c = cum[:, c * H * dk:(c + 1) * H * dk]
        last = cum_c[0:1, :] if reverse else cum_c[C - 1:C, :]
        qc = q_ref[rows, :].astype(F32) * (GLA_HK ** -0.5)
        kc = k_ref[rows, :].astype(F32)
        q_e = (qc * jnp.exp(cum_c)).astype(BF16)
        k_e = (kc * jnp.exp(-cum_c)).astype(BF16)
        k_end = (kc * jnp.exp(last - cum_c)).astype(BF16)
        decay = jnp.exp(last)
        vc = [v_ref[rows, vcols[h]] for h in heads]
        sc = [lax.dot_general(q_e[:, kcols[h]], k_e[:, kcols[h]], NT_DIMS, preferred_element_type=F32)
              for h in heads]
        sc = [jnp.where(keep, s, 0.0).astype(BF16) for s in sc]
        st = [st_ref[h] for h in heads]
        o = [jnp.dot(sc[h], vc[h], preferred_element_type=F32)
             + lax.dot_general(q_e[:, kcols[h]], st[h].astype(BF16), NT_DIMS, preferred_element_type=F32)
             for h in heads]
        for h in heads:
            st_ref[h] = (st[h] * decay[:, kcols[h]]
                         + jnp.dot(vc[h].T, k_end[:, kcols[h]], preferred_element_type=F32))
        for h in heads:
            if final:
                oh = o[h] + of_ref[rows, vcols[h]].astype(F32)
                y = _rms(oh, gn_ref[:, vcols[h]])
                r = r_ref[rows, vcols[h]].astype(F32)
                o_ref[rows, vcols[h]] = (y * (r * jax.nn.sigmoid(r))).astype(o_ref.dtype)
            else:
                o_ref[rows, vcols[h]] = o[h].astype(o_ref.dtype)


def gla_direction(proj, gates, wg2, bg, *, batch, tb, reverse, o_fwd=None, out_gain=None, cast_next=()):
    T = proj.shape[0]
    S = T // batch
    nb = S // tb
    final = o_fwd is not None
    H, dk, dv = GLA_HEADS, GLA_HK, GLA_HV
    wk, wv = H * dk, H * dv

    def tok(b, i):
        return b * nb + ((nb - 1 - i) if reverse else i)

    in_specs = [
        pl.BlockSpec((tb, wk), lambda b, i: (tok(b, i), 0)),
        pl.BlockSpec((tb, wk), lambda b, i: (tok(b, i), 1)),
        pl.BlockSpec((tb, wv), lambda b, i: (tok(b, i), (2 * wk) // wv)),
        pl.BlockSpec((tb, gates.shape[1]), lambda b, i: (tok(b, i), 0)),
        pl.BlockSpec((GLA_RANK, wk), lambda b, i: (0, 0)),
        pl.BlockSpec((1, wk), lambda b, i: (0, 0)),
    ]
    args = [proj, proj, proj, gates, wg2, bg]
    if final:
        in_specs += [
            pl.BlockSpec((tb, wv), lambda b, i: (tok(b, i), (2 * wk) // wv + 1)),
            pl.BlockSpec((tb, wv), lambda b, i: (tok(b, i), 0)),
            pl.BlockSpec((1, wv), lambda b, i: (0, 0)),
        ]
        args += [proj, o_fwd, out_gain]
    kern = functools.partial(_gla_kernel, reverse=reverse, final=final,
                             gate_off=GLA_RANK if reverse else 0)
    c_in, c_out, c_shapes, c_args = _cast_streams(cast_next, batch * nb, lambda b, i: b * nb + i)
    outs = pl.pallas_call(
        _with_cast_streams(kern, len(args), 1, len(c_args)),
        grid=(batch, nb),
        in_specs=in_specs + c_in,
        out_specs=[pl.BlockSpec((tb, wv), lambda b, i: (tok(b, i), 0))] + c_out,
        out_shape=[jax.ShapeDtypeStruct((T, wv), BF16)] + c_shapes,
        scratch_shapes=[pltpu.VMEM((H, dv, dk), F32)],
        compiler_params=_cparams(("arbitrary", "arbitrary")),
        name="gla_bwd" if reverse else "gla_fwd",
    )(*args, *c_args)
    return outs[0], outs[1:]


NA_QROWS = 8
NA_DR = 2 * NA_ROWS - 1
NA_DC = 2 * NA_COLS - 1


def _natten_bias_kernel(rpb_ref, o_ref):
    W = GRID_W
    qc = lax.broadcasted_iota(jnp.int32, (W, 2 * W), 0)
    ln = lax.broadcasted_iota(jnp.int32, (W, 2 * W), 1)
    kc = ln & (W - 1)
    dc_idx = jnp.clip(kc - qc, -(NA_COLS - 1), NA_COLS - 1) + (NA_COLS - 1)
    cs = jnp.clip(qc - NA_COLS // 2, 0, W - NA_COLS)
    in_win = (kc >= cs) & (kc < cs + NA_COLS)
    left = ln < W
    for h in range(o_ref.shape[0]):
        toeplitz = []
        for dr in range(NA_DR):
            row = jnp.broadcast_to(rpb_ref[h, dr:dr + 1, :], (W, 2 * W))
            t = jnp.take_along_axis(row, dc_idx, axis=1)
            toeplitz.append(jnp.where(in_win, t, MASK_VALUE))
        for dr0 in range(NA_ROWS):
            for jp in range(NA_ROWS // 2):
                o_ref[h, dr0, :, jp * 2 * W:(jp + 1) * 2 * W] = jnp.where(
                    left, toeplitz[dr0 + 2 * jp], toeplitz[dr0 + 2 * jp + 1])


def natten_bias_tables(rpb):
    L, H = rpb.shape[:2]
    W = GRID_W
    rpb = jnp.pad(rpb, ((0, 0), (0, 0), (0, 0), (0, 2 * W - NA_DC)))
    return pl.pallas_call(
        _natten_bias_kernel,
        grid=(L,),
        in_specs=[pl.BlockSpec((None, H, NA_DR, 2 * W), lambda l: (l, 0, 0, 0))],
        out_specs=pl.BlockSpec((None, H, NA_ROWS, W, NA_ROWS * W), lambda l: (l, 0, 0, 0, 0)),
        out_shape=jax.ShapeDtypeStruct((L, H, NA_ROWS, W, NA_ROWS * W), F32),
        compiler_params=_cparams(("parallel",)),
        name="natten_bias",
    )(rpb)


def _natten_kernel(q_ref, kp_ref, kc_ref, kn_ref, vp_ref, vc_ref, vn_ref, wb_ref, go_ref, o_ref,
                   o_scr, *, rows):
    W = GRID_W
    nq = q_ref.shape[0]
    lw = 2 * NA_HD
    nwin = NA_ROWS * W

    def window(refs, start, cols):
        parts = []
        for b, ref in enumerate(refs):
            lo, hi = max(start, b * nq), min(start + nwin, (b + 1) * nq)
            if lo < hi:
                parts.append(ref[lo - b * nq:hi - b * nq, cols])
        return parts[0] if len(parts) == 1 else jnp.concatenate(parts, axis=0)

    k_refs = (kp_ref, kc_ref, kn_ref)
    v_refs = (vp_ref, vc_ref, vn_ref)
    head0 = lax.broadcasted_iota(jnp.int32, (1, lw), 1) < NA_HD
    blk = pl.program_id(1)
    nblk = rows // NA_QROWS

    def step(r0):
        for a in range(NA_QROWS):
            rs = min(max(r0 + a - NA_ROWS // 2, 0), rows - NA_ROWS)
            off = rs - (r0 - NA_QROWS)
            dr0 = rs - (r0 + a) + (NA_ROWS - 1)
            qrows = slice(a * W, (a + 1) * W)
            pairs = range(NA_HEADS // 2)
            cols = [slice(p * lw, (p + 1) * lw) for p in pairs]
            s = []
            for p in pairs:
                qa = q_ref[qrows, cols[p]]
                zero = jnp.zeros_like(qa)
                lhs = jnp.concatenate([jnp.where(head0, qa, zero), jnp.where(head0, zero, qa)], axis=0)
                sp = lax.dot_general(lhs, window(k_refs, off * W, cols[p]), NT_DIMS,
                                     preferred_element_type=F32)
                s.append(sp + jnp.concatenate([wb_ref[2 * p, dr0], wb_ref[2 * p + 1, dr0]], axis=0))
            m = [jnp.max(sp, axis=-1, keepdims=True) for sp in s]
            e = [jnp.exp(sp - mp) for sp, mp in zip(s, m)]
            l = [jnp.sum(ep, axis=-1, keepdims=True) for ep in e]
            r = [jnp.dot(e[p].astype(BF16), window(v_refs, off * W, cols[p]), preferred_element_type=F32)
                 for p in pairs]
            for p in pairs:
                rp = r[p] / l[p]
                o_scr[qrows, cols[p]] = jnp.where(head0, rp[0:W], rp[W:2 * W])

    pl.when(blk == 0)(lambda: step(0))
    pl.when(jnp.logical_and(blk > 0, blk < nblk - 1))(lambda: step(NA_QROWS))
    pl.when(blk == nblk - 1)(lambda: step(rows - NA_QROWS))

    gw = V7X_MXU_WIDTH
    ones_bd = _head_pair_ones(gw)
    for g in range(o_ref.shape[1] // gw):
        cols = slice(g * gw, (g + 1) * gw)
        o_ref[:, cols] = _head_pair_rms(o_scr[:, cols], go_ref[:, cols], ones_bd).astype(o_ref.dtype)


def natten(proj, wb, go, *, layer, batch, col0, cast_next=()):
    T = proj.shape[0]
    S = T // batch
    rows = S // GRID_W
    nq = NA_QROWS * GRID_W
    nblk = S // nq
    width = NA_HEADS * NA_HD
    cq = col0 // width

    def prev(i):
        return jnp.maximum(i - 1, 0)

    def nxt(i):
        return jnp.minimum(i + 1, nblk - 1)

    in_specs = [
        pl.BlockSpec((nq, width), lambda b, i: (b * nblk + i, cq)),
        pl.BlockSpec((nq, width), lambda b, i: (b * nblk + prev(i), cq + 1)),
        pl.BlockSpec((nq, width), lambda b, i: (b * nblk + i, cq + 1)),
        pl.BlockSpec((nq, width), lambda b, i: (b * nblk + nxt(i), cq + 1)),
        pl.BlockSpec((nq, width), lambda b, i: (b * nblk + prev(i), cq + 2)),
        pl.BlockSpec((nq, width), lambda b, i: (b * nblk + i, cq + 2)),
        pl.BlockSpec((nq, width), lambda b, i: (b * nblk + nxt(i), cq + 2)),
        pl.BlockSpec((None,) + wb.shape[1:], lambda b, i: (layer, 0, 0, 0, 0), pipeline_mode=pl.Buffered(1)),
        pl.BlockSpec((1, width), lambda b, i: (0, 0)),
    ]
    args = [proj, proj, proj, proj, proj, proj, proj, wb, go]
    c_in, c_out, c_shapes, c_args = _cast_streams(cast_next, batch * nblk, lambda b, i: b * nblk + i)
    outs = pl.pallas_call(
        _with_cast_streams(functools.partial(_natten_kernel, rows=rows), len(args), 1, len(c_args)),
        grid=(batch, nblk),
        in_specs=in_specs + c_in,
        out_specs=[pl.BlockSpec((nq, width), lambda b, i: (b * nblk + i, 0))] + c_out,
        out_shape=[jax.ShapeDtypeStruct((T, width), BF16)] + c_shapes,
        scratch_shapes=[pltpu.VMEM((nq, width), F32)],
        compiler_params=_cparams(("arbitrary", "arbitrary")),
        name="natten",
    )(*args, *c_args)
    return outs[0], outs[1:]


def _mem_kv_kernel(mem_ref, g_ref, w_ref, gk_ref, k_ref, v_ref):
    mn = _rms(mem_ref[...], g_ref[...]).astype(BF16)
    kv = jnp.dot(mn, w_ref[...].astype(BF16), preferred_element_type=F32)
    width = k_ref.shape[1]
    for h in range(MEM_HEADS):
        cols = slice(h * MEM_HD, (h + 1) * MEM_HD)
        k_ref[:, cols] = _rms(kv[:, cols], gk_ref[...]).astype(BF16)
    v_ref[...] = kv[:, width:].astype(BF16)


def mem_kv(mem2d, gain, wkv, gk):
    n, D = mem2d.shape
    L = wkv.shape[0]
    width = wkv.shape[2] // 2
    return pl.pallas_call(
        _mem_kv_kernel,
        grid=(L,),
        in_specs=[
            pl.BlockSpec((n, D), lambda l: (0, 0)),
            pl.BlockSpec((None, 1, D), lambda l: (l, 0, 0)),
            pl.BlockSpec((None, D, 2 * width), lambda l: (l, 0, 0)),
            pl.BlockSpec((None, 1, MEM_HD), lambda l: (l, 0, 0)),
        ],
        out_specs=[pl.BlockSpec((None, n, width), lambda l: (l, 0, 0)),
                   pl.BlockSpec((None, n, width), lambda l: (l, 0, 0))],
        out_shape=[jax.ShapeDtypeStruct((L, n, width), BF16), jax.ShapeDtypeStruct((L, n, width), BF16)],
        compiler_params=_cparams(("parallel",)),
        name="mem_kv",
    )(mem2d, gain, wkv, gk)


def _out_proj_kernel(x_ref, ya_ref, yb_ref, yc_ref, w_ref, g_ref, o_ref, h_ref, *, sub):
    tm = x_ref.shape[0]
    wa, wb = ya_ref.shape[1], yb_ref.shape[1]
    for s in range(tm // sub):
        rows = slice(s * sub, (s + 1) * sub)
        acc = x_ref[rows, :] + jnp.dot(ya_ref[rows, :], w_ref[0:wa, :], preferred_element_type=F32)
        acc += jnp.dot(yb_ref[rows, :], w_ref[wa:wa + wb, :], preferred_element_type=F32)
        acc += jnp.dot(yc_ref[rows, :], w_ref[wa + wb:, :], preferred_element_type=F32)
        o_ref[rows, :] = acc
        h_ref[rows, :] = _rms(acc, g_ref[...]).astype(BF16)


def out_proj(x, y_gla, y_na, y_mem, w_out, gain, *, tm, sub):
    T, D = x.shape
    wa, wb, wc = y_gla.shape[1], y_na.shape[1], y_mem.shape[1]
    return pl.pallas_call(
        functools.partial(_out_proj_kernel, sub=sub),
        grid=(T // tm,),
        in_specs=[
            pl.BlockSpec((tm, D), lambda i: (i, 0)),
            pl.BlockSpec((tm, wa), lambda i: (i, 0)),
            pl.BlockSpec((tm, wb), lambda i: (i, 0)),
            pl.BlockSpec((tm, wc), lambda i: (i, 0)),
            pl.BlockSpec((wa + wb + wc, D), lambda i: (0, 0), pipeline_mode=pl.Buffered(1)),
            pl.BlockSpec((1, D), lambda i: (0, 0)),
        ],
        out_specs=[pl.BlockSpec((tm, D), lambda i: (i, 0)), pl.BlockSpec((tm, D), lambda i: (i, 0))],
        out_shape=[jax.ShapeDtypeStruct((T, D), F32), jax.ShapeDtypeStruct((T, D), BF16)],
        compiler_params=_cparams(("parallel",)),
        name="out_proj",
    )(x, y_gla, y_na, y_mem, w_out, gain)


def _ffn_kernel(x_ref, h_ref, w1_ref, w3_ref, w2_ref, o_ref):
    def step(acc_ref):
        h = h_ref[...]
        gate = jnp.dot(h, w1_ref[...], preferred_element_type=F32)
        up = jnp.dot(h, w3_ref[...], preferred_element_type=F32)
        act = (gate * jax.nn.sigmoid(gate) * up).astype(BF16)
        o_ref[...] = acc_ref[...] + jnp.dot(act, w2_ref[...], preferred_element_type=F32)

    pl.when(pl.program_id(1) == 0)(lambda: step(x_ref))
    pl.when(pl.program_id(1) != 0)(lambda: step(o_ref))


def ffn(x, h, w13, w2, *, tm, tf):
    T, D = x.shape
    F = w2.shape[0]
    nf = F // tf
    return pl.pallas_call(
        _ffn_kernel,
        grid=(T // tm, nf),
        in_specs=[
            pl.BlockSpec((tm, D), lambda i, f: (i, 0)),
            pl.BlockSpec((tm, D), lambda i, f: (i, 0)),
            pl.BlockSpec((D, tf), lambda i, f: (0, f)),
            pl.BlockSpec((D, tf), lambda i, f: (0, nf + f)),
            pl.BlockSpec((tf, D), lambda i, f: (f, 0)),
        ],
        out_specs=pl.BlockSpec((tm, D), lambda i, f: (i, 0)),
        out_shape=jax.ShapeDtypeStruct((T, D), F32),
        compiler_params=_cparams(("parallel", "arbitrary")),
        name="ffn",
    )(x, h, w13, w13, w2)


IN_PROJ_TN = 1024


def kernel(x, mem, attn_norm, w_in, gla_wg2_f, gla_bg_f, gla_wg2_b, gla_bg_b, gla_out_norm,
           na_q_norm, na_k_norm, na_rpb, na_out_norm, mem_norm, mem_wkv, mem_q_norm, mem_k_norm,
           mem_out_norm, w_out, ffn_norm, ffn_w13, ffn_w2):
    B, S, D = x.shape
    depth = w_in.shape[0]
    T = B * S
    gla_dk = GLA_HEADS * GLA_HK
    gla_dv = GLA_HEADS * GLA_HV
    na_w = NA_HEADS * NA_HD
    mem_w = MEM_HEADS * MEM_HD
    gate0 = 2 * gla_dk + 2 * gla_dv
    gate1 = gate0 + 2 * GLA_RANK
    na_col0 = gate0
    mem_col0 = gate0 + 3 * na_w
    assert w_in.shape[2] - 2 * GLA_RANK == mem_col0 + mem_w and gla_dv + na_w + mem_w == w_out.shape[1]
    assert na_col0 % IN_PROJ_TN == 0 and 2 * na_w == IN_PROJ_TN

    w_in_t = jnp.swapaxes(w_in, 1, 2)
    bf = {("w_in_t", 0): w_in_t[0].astype(BF16)}
    gqk = jnp.concatenate([jnp.tile(na_q_norm * (NA_HD ** -0.5), (1, NA_HEADS)),
                           jnp.tile(na_k_norm, (1, NA_HEADS))], axis=1)
    wb = natten_bias_tables(na_rpb)

    xf = x.reshape(T, D)
    k_m, v_m = mem_kv(mem.reshape(B * mem.shape[1], D), mem_norm[:, None], mem_wkv, mem_k_norm[:, None])
    stacked = {"w_in_t": w_in_t, "w_out": w_out, "w13": ffn_w13, "w2": ffn_w2}
    for l in range(depth):
        def streams(*names):
            layers = ([0] if l == 0 else []) + ([l + 1] if l + 1 < depth else [])
            keys = [(n, j) for j in layers for n in names if (n, j) not in bf]
            return keys, [(stacked[n], j) for n, j in keys]

        keys, cast = streams("w_out", "w_in_t")
        proj, gates, y_mem, done = in_proj(
            xf, attn_norm[l][None], bf["w_in_t", l], gqk[l][None], k_m, v_m, mem_q_norm[l][None],
            mem_out_norm[l][None], layer=l, batch=B, tm=512, sub=256, tn=IN_PROJ_TN,
            qk_tile=na_col0 // IN_PROJ_TN, gate0=gate0, ngate=gate1 - gate0, cast_next=cast)
        bf.update(zip(keys, done))
        keys, cast = streams("w2")
        o_f, done = gla_direction(proj, gates, gla_wg2_f[l], gla_bg_f[l][None], batch=B, tb=1024,
                                  reverse=False, cast_next=cast)
        bf.update(zip(keys, done))
        y_gla, _ = gla_direction(proj, gates, gla_wg2_b[l], gla_bg_b[l][None], batch=B, tb=1024,
                                 reverse=True, o_fwd=o_f, out_gain=gla_out_norm[l][None])
        keys, cast = streams("w13")
        y_na, done = natten(proj, wb, na_out_norm[l][None], layer=l, batch=B, col0=na_col0, cast_next=cast)
        bf.update(zip(keys, done))
        xf, hf = out_proj(xf, y_gla, y_na, y_mem, bf["w_out", l], ffn_norm[l][None], tm=512, sub=256)
        xf = ffn(xf, hf, bf["w13", l], bf["w2", l], tm=1024, tf=512)
    return xf.reshape(B, S, D)
```
